```python
import math
import jax, jax.numpy as jnp
from jax import lax
import numpy as np

D_MODEL = 1024
BATCH = 4
SEQ = 4096
DEPTH = 4
DEC_BATCH = 32
DEC_SEQ = 1
PAST_LEN = 8192
PAGE_SIZE = 128

N_HEADS = 8
HEAD_DIM = 64
ATT_WIDTH = N_HEADS * 2 * HEAD_DIM
CONV_CH = D_MODEL
CONV_K = 31
CONV_STATE = CONV_K - 1
PLE_DIM = 256
ROPE_THETA = 10000.0
Q_BLOCK = 128
EPS = 1e-6
IN_SPLITS = (ATT_WIDTH, ATT_WIDTH, ATT_WIDTH, ATT_WIDTH, CONV_CH, CONV_CH, CONV_CH, D_MODEL, D_MODEL)
IN_WIDTH = 4 * ATT_WIDTH + 3 * CONV_CH + 2 * D_MODEL

kernel_name = "hybrid_diffattn_conformer_conv_decode_step"


def rmsnorm(x, w):
    xf = x.astype(jnp.float32)
    y = xf * lax.rsqrt(jnp.mean(xf * xf, axis=-1, keepdims=True) + EPS)
    return (y * w.astype(jnp.float32)).astype(x.dtype)


def layernorm(x, g, b):
    xf = x.astype(jnp.float32)
    mu = jnp.mean(xf, axis=-1, keepdims=True)
    var = jnp.mean(jnp.square(xf - mu), axis=-1, keepdims=True)
    y = (xf - mu) * lax.rsqrt(var + EPS)
    return (y * g.astype(jnp.float32) + b.astype(jnp.float32)).astype(x.dtype)


def split_cols(z):
    out, start = [], 0
    for w in IN_SPLITS:
        out.append(z[..., start:start + w])
        start += w
    return out


def rope(x, pos):
    inv = 1.0 / (ROPE_THETA ** (jnp.arange(0, HEAD_DIM, 2, dtype=jnp.float32) / HEAD_DIM))
    ang = pos.astype(jnp.float32)[:, None] * inv[None, :]
    ang = jnp.concatenate([ang, ang], axis=-1)
    cos = jnp.cos(ang)[:, None, None, :].astype(x.dtype)
    sin = jnp.sin(ang)[:, None, None, :].astype(x.dtype)
    x1, x2 = jnp.split(x, 2, axis=-1)
    return x * cos + jnp.concatenate([-x2, x1], axis=-1) * sin


def diff_attn_prompt(q, k, v, lam):
    B, S = q.shape[0], q.shape[1]
    scale = HEAD_DIM ** -0.5
    kpos = jnp.arange(S)

    def block(i):
        q0 = i * Q_BLOCK
        qb = lax.dynamic_slice_in_dim(q, q0, Q_BLOCK, axis=1)
        s = jnp.einsum('bqhcd,bkhcd->bchqk', qb, k).astype(jnp.float32) * scale
        qpos = q0 + jnp.arange(Q_BLOCK)
        s = jnp.where(kpos[None, :] <= qpos[:, None], s, -jnp.inf)
        pr = jax.nn.softmax(s, axis=-1)
        a = pr[:, 0] - lam * pr[:, 1]
        return jnp.einsum('bhqk,bkhe->bqhe', a.astype(v.dtype), v)

    o = lax.map(block, jnp.arange(S // Q_BLOCK))
    return jnp.moveaxis(o, 0, 1).reshape(B, S, N_HEADS, 2 * HEAD_DIM)


def diff_attn_sample(q, k_new, v_new, cache_k, cache_v, page_table, l, lam):
    T = q.shape[1]
    past = page_table.shape[1] * PAGE_SIZE
    scale = HEAD_DIM ** -0.5
    kpos = jnp.arange(past + T)
    qpos = past + jnp.arange(T)
    mask = kpos[None, :] <= qpos[:, None]

    def one(args):
        qb, kb, vb, pt = args
        kc = cache_k[l, pt].reshape(past, N_HEADS, 2, HEAD_DIM)
        vc = cache_v[l, pt].reshape(past, N_HEADS, 2 * HEAD_DIM)
        kk = jnp.concatenate([kc, kb], axis=0)
        vv = jnp.concatenate([vc, vb], axis=0)
        s = jnp.einsum('qhcd,khcd->chqk', qb, kk).astype(jnp.float32) * scale
        s = jnp.where(mask, s, -jnp.inf)
        pr = jax.nn.softmax(s, axis=-1)
        a = pr[0] - lam * pr[1]
        return jnp.einsum('hqk,khe->qhe', a.astype(vv.dtype), vv)

    return lax.map(one, (q, k_new, v_new, page_table))


def depthwise_causal_conv(u_full, w, b):
    y = lax.conv_general_dilated(u_full, w[:, None, :], window_strides=(1,), padding='VALID',
                                 dimension_numbers=('NWC', 'WIO', 'NWC'),
                                 feature_group_count=u_full.shape[-1])
    return y + b


def trunk(x, p, pos, attn_fn, conv_fn, weights):
    (norm_pre, norm_post, w_in, lambda_q1, lambda_k1, lambda_q2, lambda_k2, subln_w,
     w_attn_proj, conv_w, conv_b, conv_ln_g, conv_ln_b, w_conv_proj, w_out, w_ple, w_ple_gate) = weights
    B, S, _ = x.shape
    ks, vs, cs = [], [], []
    for l in range(DEPTH):
        lam_init = 0.8 - 0.6 * math.exp(-0.3 * l)
        h = rmsnorm(x, norm_pre[l])
        q, k, v, g_att, u_a, u_b, g_conv, g_A, g_B = split_cols(h @ w_in[l])
        q = rope(q.reshape(B, S, N_HEADS, 2, HEAD_DIM), pos)
        k = rope(k.reshape(B, S, N_HEADS, 2, HEAD_DIM), pos)
        v = v.reshape(B, S, N_HEADS, 2 * HEAD_DIM)
        lam = (jnp.exp(jnp.sum(lambda_q1[l].astype(jnp.float32) * lambda_k1[l].astype(jnp.float32)))
               - jnp.exp(jnp.sum(lambda_q2[l].astype(jnp.float32) * lambda_k2[l].astype(jnp.float32)))
               + lam_init)
        o = attn_fn(l, q, k, v, lam)
        o = rmsnorm(o, subln_w[l]) * (1.0 - lam_init)
        o_att = (o.reshape(B, S, ATT_WIDTH) * jax.nn.silu(g_att)) @ w_attn_proj[l]
        u = u_a * jax.nn.sigmoid(u_b)
        c, c_state = conv_fn(l, u, conv_w[l], conv_b[l])
        c = jax.nn.silu(layernorm(c, conv_ln_g[l], conv_ln_b[l]))
        o_conv = (c * jax.nn.silu(g_conv)) @ w_conv_proj[l]
        m = jax.nn.sigmoid(g_A) * o_att + jax.nn.sigmoid(g_B) * o_conv
        x = x + rmsnorm(m @ w_out[l], norm_post[l])
        x = x + (p[l] @ w_ple[l]) * jax.nn.sigmoid(x @ w_ple_gate[l])
        ks.append(k.reshape(B, S, N_HEADS, 2 * HEAD_DIM))
        vs.append(v)
        cs.append(c_state)
    return x, jnp.stack(ks), jnp.stack(vs), jnp.stack(cs)


def setup_inputs(seed: int = 0) -> dict:
    key = jax.random.key(seed)
    ks = jax.random.split(key, 32)
    f32 = jnp.float32
    n_pages = PAST_LEN // PAGE_SIZE
    used = DEC_BATCH * n_pages
    n_pool = used + max(1, used // 4)
    nrm = lambda k, shape, s: jax.random.normal(k, shape, f32) * s
    page_table = jax.random.permutation(ks[0], n_pool)[:used].reshape(DEC_BATCH, n_pages).astype(jnp.int32)
    return {
        "x_prompt": nrm(ks[1], (BATCH, SEQ, D_MODEL), 1.0),
        "x_sample": nrm(ks[2], (DEC_BATCH, DEC_SEQ, D_MODEL), 1.0),
        "cache_k": nrm(ks[3], (DEPTH, n_pool, PAGE_SIZE, N_HEADS, 2 * HEAD_DIM), 1.0),
        "cache_v": nrm(ks[4], (DEPTH, n_pool, PAGE_SIZE, N_HEADS, 2 * HEAD_DIM), 1.0),
        "state_conv": nrm(ks[5], (DEPTH, DEC_BATCH, CONV_STATE, CONV_CH), 0.5),
        "page_table": page_table,
        "p_prompt": nrm(ks[6], (DEPTH, BATCH, SEQ, PLE_DIM), 1.0),
        "p_sample": nrm(ks[7], (DEPTH, DEC_BATCH, DEC_SEQ, PLE_DIM), 1.0),
        "norm_pre": 1.0 + nrm(ks[8], (DEPTH, D_MODEL), 0.02),
        "norm_post": 1.0 + nrm(ks[9], (DEPTH, D_MODEL), 0.02),
        "w_in": nrm(ks[10], (DEPTH, D_MODEL, IN_WIDTH), D_MODEL ** -0.5),
        "lambda_q1": nrm(ks[11], (DEPTH, HEAD_DIM), 0.1),
        "lambda_k1": nrm(ks[12], (DEPTH, HEAD_DIM), 0.1),
        "lambda_q2": nrm(ks[13], (DEPTH, HEAD_DIM), 0.1),
        "lambda_k2": nrm(ks[14], (DEPTH, HEAD_DIM), 0.1),
        "subln_w": 1.0 + nrm(ks[15], (DEPTH, 2 * HEAD_DIM), 0.02),
        "w_attn_proj": nrm(ks[16], (DEPTH, ATT_WIDTH, D_MODEL), ATT_WIDTH ** -0.5),
        "conv_w": nrm(ks[17], (DEPTH, CONV_K, CONV_CH), CONV_K ** -0.5),
        "conv_b": nrm(ks[18], (DEPTH, CONV_CH), 0.01),
        "conv_ln_g": 1.0 + nrm(ks[19], (DEPTH, CONV_CH), 0.02),
        "conv_ln_b": nrm(ks[20], (DEPTH, CONV_CH), 0.01),
        "w_conv_proj": nrm(ks[21], (DEPTH, CONV_CH, D_MODEL), CONV_CH ** -0.5),
        "w_out": nrm(ks[22], (DEPTH, D_MODEL, D_MODEL), D_MODEL ** -0.5),
        "w_ple": nrm(ks[23], (DEPTH, PLE_DIM, D_MODEL), PLE_DIM ** -0.5),
        "w_ple_gate": nrm(ks[24], (DEPTH, D_MODEL, D_MODEL), D_MODEL ** -0.5),
    }


def reference(x_prompt, x_sample, cache_k, cache_v, state_conv, page_table, p_prompt, p_sample,
              norm_pre, norm_post, w_in, lambda_q1, lambda_k1, lambda_q2, lambda_k2, subln_w,
              w_attn_proj, conv_w, conv_b, conv_ln_g, conv_ln_b, w_conv_proj, w_out, w_ple, w_ple_gate):
    weights = (norm_pre, norm_post, w_in, lambda_q1, lambda_k1, lambda_q2, lambda_k2, subln_w,
               w_attn_proj, conv_w, conv_b, conv_ln_g, conv_ln_b, w_conv_proj, w_out, w_ple, w_ple_gate)

    def attn_prompt(l, q, k, v, lam):
        return diff_attn_prompt(q, k, v, lam)

    def conv_prompt_fn(l, u, w, b):
        u_full = jnp.pad(u, ((0, 0), (CONV_STATE, 0), (0, 0)))
        return depthwise_causal_conv(u_full, w, b), u_full[:, -CONV_STATE:]

    pos_prompt = jnp.arange(x_prompt.shape[1], dtype=jnp.int32)
    y_prompt, k_prompt, v_prompt, conv_prompt = trunk(x_prompt, p_prompt, pos_prompt,
                                                      attn_prompt, conv_prompt_fn, weights)

    past = page_table.shape[1] * PAGE_SIZE

    def attn_sample(l, q, k, v, lam):
        return diff_attn_sample(q, k, v, cache_k, cache_v, page_table, l, lam)

    def conv_sample_fn(l, u, w, b):
        u_full = jnp.concatenate([state_conv[l].astype(u.dtype), u], axis=1)
        return depthwise_causal_conv(u_full, w, b), u_full[:, -CONV_STATE:]

    pos_sample = past + jnp.arange(x_sample.shape[1], dtype=jnp.int32)
    y_sample, k_sample, v_sample, conv_sample = trunk(x_sample, p_sample, pos_sample,
                                                      attn_sample, conv_sample_fn, weights)
    return (y_prompt, y_sample, k_prompt, v_prompt, conv_prompt, k_sample, v_sample, conv_sample)
```

```python
import functools
import math

import jax
import jax.numpy as jnp
from jax import lax
from jax.experimental import pallas as pl
from jax.experimental.pallas import tpu as pltpu

D_MODEL = 1024
N_HEADS = 8
HEAD_DIM = 64
HEAD_W = 2 * HEAD_DIM
CONV_K = 31
CONV_STATE = CONV_K - 1
PLE_DIM = 256
PAGE_SIZE = 128
ROPE_THETA = 10000.0
EPS = 1e-6
DEPTH = 4

COL_Q, COL_K, COL_V, COL_GATT, COL_UA, COL_GCONV, COL_GA, COL_GB = 0, 1, 2, 3, 4, 6, 7, 8

VMEM_LIMIT_BYTES = 56 * 1024 * 1024
CONV_HALO = 32
CONV_ROW_CHUNK = 32

f32 = jnp.float32
bf16 = jnp.bfloat16


def _params(*sem):
    return pltpu.CompilerParams(dimension_semantics=sem, vmem_limit_bytes=VMEM_LIMIT_BYTES)


def _dot(a, b):
    return jnp.dot(a, b, preferred_element_type=f32)


def _dot_nt(a, b):
    return lax.dot_general(a, b, (((1,), (1,)), ((), ())), preferred_element_type=f32)


def _sigmoid(z):
    return 1.0 / (1.0 + jnp.exp(-z))


def _silu(z):
    return z * _sigmoid(z)


def _rms(xf, w):
    return xf * lax.rsqrt(jnp.mean(xf * xf, axis=-1, keepdims=True) + EPS) * w


def _row_spec(tm, width=D_MODEL):
    return pl.BlockSpec((tm, width), lambda i: (i, 0))


def _const_spec(shape, idx=(0, 0)):
    return pl.BlockSpec(shape, lambda i: idx)


def _norm_kernel(x_ref, w_ref, h_ref):
    h_ref[...] = _rms(x_ref[...], w_ref[...]).astype(bf16)


def _prenorm(x, w, tm):
    n = x.shape[0]
    return pl.pallas_call(
        _norm_kernel,
        grid=(n // tm,),
        in_specs=[_row_spec(tm), _const_spec((1, D_MODEL))],
        out_specs=_row_spec(tm),
        out_shape=jax.ShapeDtypeStruct((n, D_MODEL), bf16),
        compiler_params=_params("parallel"),
        name="prenorm",
    )(x, w)


def _rope(z, cos_ref, sa_ref, sb_ref):
    return (z * cos_ref[...]
            + pltpu.roll(z, D_MODEL - HEAD_DIM // 2, 1) * sa_ref[...]
            + pltpu.roll(z, HEAD_DIM // 2, 1) * sb_ref[...])


def _q_kernel(h_ref, w_ref, cos_ref, sa_ref, sb_ref, q1_ref, q2_ref):
    r = _rope(_dot(h_ref[...], w_ref[...]), cos_ref, sa_ref, sb_ref) * (HEAD_DIM ** -0.5)
    lane = lax.broadcasted_iota(jnp.int32, r.shape, 1)
    first = (lane & HEAD_DIM) == 0
    q1_ref[...] = jnp.where(first, r, 0.0).astype(bf16)
    q2_ref[...] = jnp.where(first, 0.0, r).astype(bf16)


def _store_heads(o_ref, z):
    for hd in range(N_HEADS):
        o_ref[:, hd, :] = z[:, hd * HEAD_W:(hd + 1) * HEAD_W]


def _k_kernel(h_ref, w_ref, cos_ref, sa_ref, sb_ref, k_ref, kb_ref):
    r = _rope(_dot(h_ref[...], w_ref[...]), cos_ref, sa_ref, sb_ref)
    _store_heads(k_ref, r)
    kb_ref[...] = r.astype(bf16)


def _v_kernel(h_ref, w_ref, v_ref, vb_ref):
    z = _dot(h_ref[...], w_ref[...])
    _store_heads(v_ref, z)
    vb_ref[...] = z.astype(bf16)


def _silu_kernel(h_ref, w_ref, o_ref):
    o_ref[...] = _silu(_dot(h_ref[...], w_ref[...])).astype(bf16)


def _glu_kernel(h_ref, w_ref, u_ref):
    z = _dot(h_ref[...], w_ref[...])
    u_ref[...] = z[:, :D_MODEL] * _sigmoid(z[:, D_MODEL:])


def _project(kernel_fn, name, h, w_in_b, col, tm, out_dtypes, tables=None, wcols=D_MODEL, heads_first=False):
    n = h.shape[0]
    in_specs = [_row_spec(tm), _const_spec((D_MODEL, wcols), (0, col * D_MODEL // wcols))]
    args = [h, w_in_b]
    if tables is not None:
        tb = tables[0].shape[0] // tm
        in_specs += [pl.BlockSpec((tm, D_MODEL), lambda i: (i % tb, 0))] * 3
        args += list(tables)
    per_head = [heads_first and j == 0 for j in range(len(out_dtypes))]
    head_spec = pl.BlockSpec((tm, N_HEADS, HEAD_W), lambda i: (i, 0, 0))
    outs = pl.pallas_call(
        kernel_fn,
        grid=(n // tm,),
        in_specs=in_specs,
        out_specs=[head_spec if ph else _row_spec(tm) for ph in per_head],
        out_shape=[jax.ShapeDtypeStruct((n, N_HEADS, HEAD_W) if ph else (n, D_MODEL), dt)
                   for ph, dt in zip(per_head, out_dtypes)],
        compiler_params=_params("parallel"),
        name=name,
    )(*args)
    return outs


def _lambda(lq1_ref, lk1_ref, lq2_ref, lk2_ref, lam_init):
    a = jnp.sum(lq1_ref[...] * lk1_ref[...], axis=-1, keepdims=True)
    b = jnp.sum(lq2_ref[...] * lk2_ref[...], axis=-1, keepdims=True)
    return jnp.exp(a) - jnp.exp(b) + lam_init


def _attn_kernel(q1_ref, q2_ref, k_ref, v_ref, g_ref, sw_ref, lq1_ref, lk1_ref, lq2_ref, lk2_ref,
                 o_ref, m_ref, l_ref, acc_ref, *, lam_init):
    qi = pl.program_id(2)
    ki = pl.program_id(3)

    @pl.when(ki == 0)
    def _():
        m_ref[...] = jnp.full(m_ref.shape, -jnp.inf, f32)
        l_ref[...] = jnp.zeros(l_ref.shape, f32)
        acc_ref[...] = jnp.zeros(acc_ref.shape, f32)

    def step(masked):
        k = k_ref[...]
        v = v_ref[...]
        for c, q_ref in enumerate((q1_ref, q2_ref)):
            s = _dot_nt(q_ref[...], k)
            if masked:
                row = lax.broadcasted_iota(jnp.int32, s.shape, 0)
                col = lax.broadcasted_iota(jnp.int32, s.shape, 1)
                s = jnp.where(col <= row, s, -jnp.inf)
            m_prev = m_ref[c]
            m_new = jnp.maximum(m_prev, jnp.max(s, axis=-1, keepdims=True))
            alpha = jnp.exp(m_prev - m_new)
            p = jnp.exp(s - m_new)
            l_ref[c] = alpha * l_ref[c] + jnp.sum(p, axis=-1, keepdims=True)
            acc_ref[c] = alpha * acc_ref[c] + _dot(p.astype(bf16), v)
            m_ref[c] = m_new

    @pl.when(ki < qi)
    def _():
        step(False)

    @pl.when(ki == qi)
    def _():
        step(True)
        lam = _lambda(lq1_ref, lk1_ref, lq2_ref, lk2_ref, lam_init)
        o = acc_ref[0] / l_ref[0] - lam * (acc_ref[1] / l_ref[1])
        o = _rms(o, sw_ref[...]) * (1.0 - lam_init)
        o_ref[...] = (o * g_ref[...].astype(f32)).astype(bf16)


def _prompt_attention(q1, q2, kb, vb, g, sw, lams, lam_init, batch, seq, tq):
    nq = seq // tq
    qspec = pl.BlockSpec((tq, HEAD_W), lambda b, h, qi, ki: (b * nq + qi, h))
    kspec = pl.BlockSpec((tq, HEAD_W), lambda b, h, qi, ki: (b * nq + jnp.minimum(ki, qi), h))
    small = lambda w: pl.BlockSpec((1, w), lambda b, h, qi, ki: (0, 0))
    return pl.pallas_call(
        functools.partial(_attn_kernel, lam_init=lam_init),
        grid=(batch, N_HEADS, nq, nq),
        in_specs=[qspec, qspec, kspec, kspec, qspec, small(HEAD_W)] + [small(HEAD_DIM)] * 4,
        out_specs=qspec,
        out_shape=jax.ShapeDtypeStruct((batch * seq, D_MODEL), bf16),
        scratch_shapes=[pltpu.VMEM((2, tq, 1), f32), pltpu.VMEM((2, tq, 1), f32),
                        pltpu.VMEM((2, tq, HEAD_W), f32)],
        compiler_params=_params("parallel", "parallel", "parallel", "arbitrary"),
        name="prompt_attention",
    )(q1, q2, kb, vb, g, sw, *lams)


def _decode_kernel(pt_ref, q_ref, kn_ref, vn_ref, g_ref, sw_ref, lq1_ref, lk1_ref, lq2_ref, lk2_ref,
                   kp_ref, vp_ref, o_ref, m_ref, l_ref, acc_ref, *, lam_init, n_pages):
    p_idx = pl.program_id(1)
    rows = PAGE_SIZE * N_HEADS

    @pl.when(p_idx == 0)
    def _():
        m_ref[...] = jnp.full(m_ref.shape, -jnp.inf, f32)
        l_ref[...] = jnp.zeros(l_ref.shape, f32)
        acc_ref[...] = jnp.zeros(acc_ref.shape, f32)

    k2 = kp_ref[...].reshape(rows, HEAD_W).astype(bf16)
    v2 = vp_ref[...].reshape(rows, HEAD_W).astype(bf16)
    s = _dot_nt(q_ref[...].astype(bf16), k2)
    r_head = lax.broadcasted_iota(jnp.int32, s.shape, 0) % N_HEADS
    c_head = lax.broadcasted_iota(jnp.int32, s.shape, 1) % N_HEADS
    s = jnp.where(r_head == c_head, s, -jnp.inf)
    m_prev = m_ref[...]
    m_new = jnp.maximum(m_prev, jnp.max(s, axis=-1, keepdims=True))
    alpha = jnp.exp(m_prev - m_new)
    p = jnp.exp(s - m_new)
    l_ref[...] = alpha * l_ref[...] + jnp.sum(p, axis=-1, keepdims=True)
    acc_ref[...] = alpha * acc_ref[...] + _dot(p.astype(bf16), v2)
    m_ref[...] = m_new

    @pl.when(p_idx == n_pages - 1)
    def _():
        kn = kn_ref[...]
        vn = vn_ref[...]
        s_n = jnp.sum(q_ref[...] * jnp.concatenate([kn, kn], axis=0), axis=-1, keepdims=True)
        m_prev = m_ref[...]
        m_new = jnp.maximum(m_prev, s_n)
        alpha = jnp.exp(m_prev - m_new)
        p_n = jnp.exp(s_n - m_new)
        l = alpha * l_ref[...] + p_n
        acc = (alpha * acc_ref[...] + p_n * jnp.concatenate([vn, vn], axis=0)) / l
        lam = _lambda(lq1_ref, lk1_ref, lq2_ref, lk2_ref, lam_init)
        o = acc[:N_HEADS] - lam * acc[N_HEADS:]
        o = _rms(o, sw_ref[...]) * (1.0 - lam_init)
        o_ref[...] = o * g_ref[...]


def _sample_attention(page_table, q, kn, vn, g, sw, lams, cache_k, cache_v, layer, lam_init):
    nb, n_pages = page_table.shape
    per_b = lambda r: pl.BlockSpec((None, r, HEAD_W), lambda b, p, pt: (b, 0, 0))
    small = lambda w: pl.BlockSpec((1, w), lambda b, p, pt: (0, 0))
    page = pl.BlockSpec((None, None, PAGE_SIZE, N_HEADS, HEAD_W),
                        lambda b, p, pt: (layer, pt[b, p], 0, 0, 0))
    grid_spec = pltpu.PrefetchScalarGridSpec(
        num_scalar_prefetch=1,
        grid=(nb, n_pages),
        in_specs=[per_b(2 * N_HEADS), per_b(N_HEADS), per_b(N_HEADS), per_b(N_HEADS), small(HEAD_W)]
                 + [small(HEAD_DIM)] * 4 + [page, page],
        out_specs=per_b(N_HEADS),
        scratch_shapes=[pltpu.VMEM((2 * N_HEADS, 1), f32), pltpu.VMEM((2 * N_HEADS, 1), f32),
                        pltpu.VMEM((2 * N_HEADS, HEAD_W), f32)],
    )
    return pl.pallas_call(
        functools.partial(_decode_kernel, lam_init=lam_init, n_pages=n_pages),
        grid_spec=grid_spec,
        out_shape=jax.ShapeDtypeStruct((nb, N_HEADS, HEAD_W), f32),
        compiler_params=_params("parallel", "arbitrary"),
        name="sample_attention",
    )(page_table, q, kn, vn, g, sw, *lams, cache_k, cache_v)


def _conv_tail(y, h_ref, wg_ref, g_ref, b_ref, o_ref):
    mu = jnp.mean(y, axis=-1, keepdims=True)
    yc = y - mu
    var = jnp.mean(yc * yc, axis=-1, keepdims=True)
    c = _silu(yc * lax.rsqrt(var + EPS) * g_ref[...] + b_ref[...])
    o_ref[...] = (c * _silu(_dot(h_ref[...], wg_ref[...]))).astype(bf16)


def _conv_kernel(u_ref, halo_ref, h_ref, wg_ref, cw_ref, cb_ref, g_ref, b_ref, o_ref,
                 full_ref, shift_ref, y_ref, *, tm, tiles_per_seq):
    i = pl.program_id(0)
    full_ref[0:CONV_HALO, :] = jnp.where(i % tiles_per_seq == 0, 0.0, halo_ref[...])
    full_ref[CONV_HALO:, :] = u_ref[...]
    span = tm + CONV_HALO - 8
    for s in range(1, 8):
        shift_ref[s - 1] = full_ref[s:s + span, :]

    def body(rb, carry):
        r0 = pl.multiple_of(rb * CONV_ROW_CHUNK, CONV_ROW_CHUNK)
        acc = jnp.broadcast_to(cb_ref[...], (CONV_ROW_CHUNK, D_MODEL))
        for k in range(CONV_K):
            off = CONV_HALO - CONV_STATE + k
            s, a = off % 8, off - off % 8
            if s == 0:
                tap = full_ref[pl.ds(r0 + a, CONV_ROW_CHUNK), :]
            else:
                tap = shift_ref[s - 1, pl.ds(r0 + a, CONV_ROW_CHUNK), :]
            acc = acc + cw_ref[k:k + 1, :] * tap
        y_ref[pl.ds(r0, CONV_ROW_CHUNK), :] = acc
        return carry

    lax.fori_loop(0, tm // CONV_ROW_CHUNK, body, 0)
    _conv_tail(y_ref[...], h_ref, wg_ref, g_ref, b_ref, o_ref)


def _prompt_conv(u, h, w_in_b, cw, cb, g, b, seq, tm):
    n = u.shape[0]
    ratio = tm // CONV_HALO
    halo = pl.BlockSpec((CONV_HALO, D_MODEL), lambda i: (jnp.maximum(i * ratio - 1, 0), 0))
    return pl.pallas_call(
        functools.partial(_conv_kernel, tm=tm, tiles_per_seq=seq // tm),
        grid=(n // tm,),
        in_specs=[_row_spec(tm), halo, _row_spec(tm), _const_spec((D_MODEL, D_MODEL), (0, COL_GCONV)),
                  _const_spec((CONV_K, D_MODEL))] + [_const_spec((1, D_MODEL))] * 3,
        out_specs=_row_spec(tm),
        out_shape=jax.ShapeDtypeStruct((n, D_MODEL), bf16),
        scratch_shapes=[pltpu.VMEM((tm + CONV_HALO, D_MODEL), f32),
                        pltpu.VMEM((7, tm + CONV_HALO - 8, D_MODEL), f32),
                        pltpu.VMEM((tm, D_MODEL), f32)],
        compiler_params=_params("parallel"),
        name="prompt_conv",
    )(u, u, h, w_in_b, cw, cb, g, b)


def _sample_conv_kernel(st_ref, u_ref, h_ref, wg_ref, cw_ref, cb_ref, g_ref, b_ref, o_ref):
    w = cw_ref[...]
    y = jnp.sum(st_ref[...] * w[None, :CONV_STATE, :], axis=1)
    y = y + u_ref[...] * w[CONV_STATE:, :] + cb_ref[...]
    _conv_tail(y, h_ref, wg_ref, g_ref, b_ref, o_ref)


def _sample_conv(state, u, h, w_in_b, cw, cb, g, b):
    nb = u.shape[0]
    return pl.pallas_call(
        _sample_conv_kernel,
        grid=(1,),
        in_specs=[pl.BlockSpec((nb, CONV_STATE, D_MODEL), lambda i: (0, 0, 0)), _row_spec(nb), _row_spec(nb),
                  _const_spec((D_MODEL, D_MODEL), (0, COL_GCONV)), _const_spec((CONV_K, D_MODEL))]
                 + [_const_spec((1, D_MODEL))] * 3,
        out_specs=_row_spec(nb),
        out_shape=jax.ShapeDtypeStruct((nb, D_MODEL), bf16),
        compiler_params=_params("arbitrary"),
        name="sample_conv",
    )(state, u, h, w_in_b, cw, cb, g, b)


def _merge_kernel(x_ref, h_ref, oa_ref, oc_ref, p_ref, wga_ref, wgb_ref, wap_ref, wcp_ref, wout_ref,
                  wple_ref, wpg_ref, npost_ref, npre_ref, xo_ref, ho_ref):
    h = h_ref[...]
    o_att = _dot(oa_ref[...], wap_ref[...])
    o_conv = _dot(oc_ref[...], wcp_ref[...])
    m = _sigmoid(_dot(h, wga_ref[...])) * o_att + _sigmoid(_dot(h, wgb_ref[...])) * o_conv
    x1 = x_ref[...] + _rms(_dot(m.astype(bf16), wout_ref[...]), npost_ref[...])
    ple = _dot(p_ref[...].astype(bf16), wple_ref[...])
    x2 = x1 + ple * _sigmoid(_dot(x1.astype(bf16), wpg_ref[...]))
    xo_ref[...] = x2
    ho_ref[...] = _rms(x2, npre_ref[...]).astype(bf16)


def _merge(x, h, oa, oc, p, w_in_b, wap, wcp, wout, wple, wpg, npost, npre_next, tm):
    n = x.shape[0]
    sq = _const_spec((D_MODEL, D_MODEL))
    return pl.pallas_call(
        _merge_kernel,
        grid=(n // tm,),
        in_specs=[_row_spec(tm)] * 4 + [_row_spec(tm, PLE_DIM),
                  _const_spec((D_MODEL, D_MODEL), (0, COL_GA)), _const_spec((D_MODEL, D_MODEL), (0, COL_GB)),
                  sq, sq, sq, _const_spec((PLE_DIM, D_MODEL)), sq,
                  _const_spec((1, D_MODEL)), _const_spec((1, D_MODEL))],
        out_specs=[_row_spec(tm), _row_spec(tm)],
        out_shape=[jax.ShapeDtypeStruct((n, D_MODEL), f32), jax.ShapeDtypeStruct((n, D_MODEL), bf16)],
        compiler_params=_params("parallel"),
        name="merge",
    )(x, h, oa, oc, p, w_in_b, w_in_b, wap, wcp, wout, wple, wpg, npost, npre_next)


def _rope_tables(pos, rows):
    inv = 1.0 / (ROPE_THETA ** (jnp.arange(0, HEAD_DIM, 2, dtype=f32) / HEAD_DIM))
    ang = pos.astype(f32)[:, None] * inv[None, :]
    ang = jnp.concatenate([ang, ang], axis=-1)
    reps = D_MODEL // HEAD_DIM
    cos = jnp.tile(jnp.cos(ang), (1, reps))
    sin = jnp.tile(jnp.sin(ang), (1, reps))
    first = (jnp.arange(D_MODEL) % HEAD_DIM) < HEAD_DIM // 2
    tables = (cos, jnp.where(first, -sin, 0.0), jnp.where(first, 0.0, sin))
    return tuple(jnp.broadcast_to(t, (rows, D_MODEL)) for t in tables)


def _trunk(x, p, tables, attn_fn, conv_fn, wts, tm):
    ks, vs, us = [], [], []
    h = _prenorm(x, wts["norm_pre"][0][None], tm)
    for l in range(DEPTH):
        w_in_b = wts["w_in"][l]
        lam_init = 0.8 - 0.6 * math.exp(-0.3 * l)
        q1, q2 = _project(_q_kernel, "proj_q", h, w_in_b, COL_Q, tm, (bf16, bf16), tables)
        k, kb = _project(_k_kernel, "proj_k", h, w_in_b, COL_K, tm, (f32, bf16), tables, heads_first=True)
        v, vb = _project(_v_kernel, "proj_v", h, w_in_b, COL_V, tm, (f32, bf16), heads_first=True)
        (g_att,) = _project(_silu_kernel, "proj_gatt", h, w_in_b, COL_GATT, tm, (bf16,))
        (u,) = _project(_glu_kernel, "proj_glu", h, w_in_b, COL_UA, tm, (f32,), wcols=2 * D_MODEL)
        lams = tuple(wts[n][l][None] for n in ("lambda_q1", "lambda_k1", "lambda_q2", "lambda_k2"))
        oa = attn_fn(l, q1, q2, k, v, kb, vb, g_att, wts["subln_w"][l][None], lams, lam_init)
        oc = conv_fn(l, u, h, w_in_b, wts["conv_w"][l], wts["conv_b"][l][None],
                     wts["conv_ln_g"][l][None], wts["conv_ln_b"][l][None])
        x, h = _merge(x, h, oa, oc, p[l], w_in_b, wts["w_attn_proj"][l], wts["w_conv_proj"][l],
                      wts["w_out"][l], wts["w_ple"][l], wts["w_ple_gate"][l],
                      wts["norm_post"][l][None], wts["norm_pre"][(l + 1) % DEPTH][None], tm)
        ks.append(k)
        vs.append(v)
        us.append(u)
    return x, jnp.stack(ks), jnp.stack(vs), us


def kernel(x_prompt, x_sample, cache_k, cache_v, state_conv, page_table, p_prompt, p_sample,
           norm_pre, norm_post, w_in, lambda_q1, lambda_k1, lambda_q2, lambda_k2, subln_w,
           w_attn_proj, conv_w, conv_b, conv_ln_g, conv_ln_b, w_conv_proj, w_out, w_ple, w_ple_gate):
    batch, seq, _ = x_prompt.shape
    nb, dec_seq, _ = x_sample.shape
    assert dec_seq == 1
    n_pages = page_table.shape[1]
    past = n_pages * PAGE_SIZE
    wts = dict(norm_pre=norm_pre, norm_post=norm_post, w_in=w_in.astype(bf16),
               lambda_q1=lambda_q1, lambda_k1=lambda_k1, lambda_q2=lambda_q2, lambda_k2=lambda_k2,
               subln_w=subln_w, w_attn_proj=w_attn_proj.astype(bf16), conv_w=conv_w, conv_b=conv_b,
               conv_ln_g=conv_ln_g, conv_ln_b=conv_ln_b, w_conv_proj=w_conv_proj.astype(bf16),
               w_out=w_out.astype(bf16), w_ple=w_ple.astype(bf16), w_ple_gate=w_ple_gate.astype(bf16))

    tm_p = 256
    tq = 512

    def attn_prompt(l, q1, q2, k, v, kb, vb, g_att, sw, lams, lam_init):
        return _prompt_attention(q1, q2, kb, vb, g_att, sw, lams, lam_init, batch, seq, tq)

    def conv_prompt_fn(l, u, h, w_in_b, cw, cb, g, b):
        return _prompt_conv(u, h, w_in_b, cw, cb, g, b, seq, tm_p)

    tables_p = _rope_tables(jnp.arange(seq, dtype=jnp.int32), seq)
    y_p, k_p, v_p, u_p = _trunk(x_prompt.reshape(batch * seq, D_MODEL),
                                p_prompt.reshape(DEPTH, batch * seq, PLE_DIM),
                                tables_p, attn_prompt, conv_prompt_fn, wts, tm_p)
    conv_p = jnp.stack([u.reshape(batch, seq, D_MODEL)[:, seq - CONV_STATE:] for u in u_p])

    def attn_sample(l, q1, q2, k, v, kb, vb, g_att, sw, lams, lam_init):
        per_head = lambda a: a.astype(f32).reshape(nb, N_HEADS, HEAD_W)
        q = jnp.concatenate([per_head(q1), per_head(q2)], axis=1)
        o = _sample_attention(page_table, q, k, v, per_head(g_att), sw, lams, cache_k, cache_v, l, lam_init)
        return o.reshape(nb, D_MODEL).astype(bf16)

    def conv_sample_fn(l, u, h, w_in_b, cw, cb, g, b):
        return _sample_conv(state_conv[l], u, h, w_in_b, cw, cb, g, b)

    tables_s = _rope_tables(past + jnp.arange(1, dtype=jnp.int32), nb)
    y_s, k_s, v_s, u_s = _trunk(x_sample.reshape(nb, D_MODEL), p_sample.reshape(DEPTH, nb, PLE_DIM),
                                tables_s, attn_sample, conv_sample_fn, wts, nb)
    conv_s = jnp.stack([jnp.concatenate([state_conv[l][:, 1:], u_s[l][:, None]], axis=1) for l in range(DEPTH)])

    hshape = (N_HEADS, HEAD_W)
    return (y_p.reshape(batch, seq, D_MODEL), y_s.reshape(nb, 1, D_MODEL),
            k_p.reshape(DEPTH, batch, seq, *hshape), v_p.reshape(DEPTH, batch, seq, *hshape), conv_p,
            k_s.reshape(DEPTH, nb, 1, *hshape), v_s.reshape(DEPTH, nb, 1, *hshape), conv_s)
```

```python
import functools
import math

import jax
import jax.numpy as jnp
from jax import lax
from jax.experimental import pallas as pl
from jax.experimental.pallas import tpu as pltpu

D_MODEL = 1024
N_HEADS = 8
HEAD_DIM = 64
HEAD_W = 2 * HEAD_DIM
CONV_K = 31
CONV_STATE = CONV_K - 1
PLE_DIM = 256
PAGE_SIZE = 128
ROPE_THETA = 10000.0
EPS = 1e-6
DEPTH = 4

COL_Q, COL_K, COL_V, COL_GATT, COL_UA, COL_GCONV, COL_GA, COL_GB = 0, 1, 2, 3, 4, 6, 7, 8

Q_SCALE = HEAD_DIM ** -0.5 * math.log2(math.e)
ATTN_TILE = 512
ROW_TILE = 256
DECODE_PAGES_PER_STEP = 8

VMEM_LIMIT_BYTES = 56 * 1024 * 1024
CONV_HALO = 32
CONV_ROW_CHUNK = 32

f32 = jnp.float32
bf16 = jnp.bfloat16


def _params(*sem):
    return pltpu.CompilerParams(dimension_semantics=sem, vmem_limit_bytes=VMEM_LIMIT_BYTES)


def _dot(a, b):
    return jnp.dot(a, b, preferred_element_type=f32)


def _dot_nt(a, b):
    return lax.dot_general(a, b, (((1,), (1,)), ((), ())), preferred_element_type=f32)


def _sigmoid(z):
    return 1.0 / (1.0 + jnp.exp(-z))


def _silu(z):
    return z * _sigmoid(z)


def _rms(xf, w):
    return xf * lax.rsqrt(jnp.mean(xf * xf, axis=-1, keepdims=True) + EPS) * w


def _row_spec(tm, width=D_MODEL):
    return pl.BlockSpec((tm, width), lambda i: (i, 0))


def _const_spec(shape, idx=(0, 0)):
    return pl.BlockSpec(shape, lambda i: idx)


def _norm_kernel(x_ref, w_ref, h_ref):
    h_ref[...] = _rms(x_ref[...], w_ref[...]).astype(bf16)


def _prenorm(x, w, tm):
    n = x.shape[0]
    return pl.pallas_call(
        _norm_kernel,
        grid=(n // tm,),
        in_specs=[_row_spec(tm), _const_spec((1, D_MODEL))],
        out_specs=_row_spec(tm),
        out_shape=jax.ShapeDtypeStruct((n, D_MODEL), bf16),
        compiler_params=_params("parallel"),
        name="prenorm",
    )(x, w)


def _rope(z, cos_ref, sa_ref, sb_ref):
    wide = lambda t_ref: jnp.concatenate([t_ref[...]] * N_HEADS, axis=1)
    return (z * wide(cos_ref)
            + pltpu.roll(z, D_MODEL - HEAD_DIM // 2, 1) * wide(sa_ref)
            + pltpu.roll(z, HEAD_DIM // 2, 1) * wide(sb_ref))


def _q_kernel(h_ref, w_ref, cos_ref, sa_ref, sb_ref, q1_ref, q2_ref, *, transposed):
    r = _rope(_dot(h_ref[...], w_ref[...]), cos_ref, sa_ref, sb_ref) * Q_SCALE
    if transposed:
        r = r.T
    idx = lax.broadcasted_iota(jnp.int32, r.shape, 0 if transposed else 1)
    first = (idx & HEAD_DIM) == 0
    q1_ref[...] = jnp.where(first, r, 0.0).astype(bf16)
    q2_ref[...] = jnp.where(first, 0.0, r).astype(bf16)


def _store_heads(o_ref, z):
    for hd in range(N_HEADS):
        o_ref[:, hd, :] = z[:, hd * HEAD_W:(hd + 1) * HEAD_W]


def _k_kernel(h_ref, w_ref, cos_ref, sa_ref, sb_ref, k_ref, kb_ref):
    r = _rope(_dot(h_ref[...], w_ref[...]), cos_ref, sa_ref, sb_ref)
    _store_heads(k_ref, r)
    kb_ref[...] = r.astype(bf16)


def _v_kernel(h_ref, w_ref, v_ref, vt_ref):
    z = _dot(h_ref[...], w_ref[...])
    _store_heads(v_ref, z)
    vt_ref[...] = z.T.astype(bf16)


def _v_rows_kernel(h_ref, w_ref, v_ref):
    _store_heads(v_ref, _dot(h_ref[...], w_ref[...]))


def _silu_kernel(h_ref, w_ref, o_ref):
    o_ref[...] = _silu(_dot(h_ref[...], w_ref[...])).astype(bf16)


def _glu_kernel(h_ref, w_ref, u_ref):
    z = _dot(h_ref[...], w_ref[...])
    u_ref[...] = z[:, :D_MODEL] * _sigmoid(z[:, D_MODEL:])


def _project(kernel_fn, name, h, w_in_b, col, tm, outs, tables=None, wcols=D_MODEL):
    n = h.shape[0]
    in_specs = [_row_spec(tm), _const_spec((D_MODEL, wcols), (0, col * D_MODEL // wcols))]
    args = [h, w_in_b]
    if tables is not None:
        tb = tables[0].shape[0] // tm
        in_specs += [pl.BlockSpec((tm, HEAD_W), lambda i: (i % tb, 0))] * 3
        args += list(tables)
    spec = dict(rows=_row_spec(tm),
                heads=pl.BlockSpec((tm, N_HEADS, HEAD_W), lambda i: (i, 0, 0)),
                cols=pl.BlockSpec((None, D_MODEL, tm), lambda i: (i, 0, 0)))
    shape = dict(rows=(n, D_MODEL), heads=(n, N_HEADS, HEAD_W), cols=(n // tm, D_MODEL, tm))
    return pl.pallas_call(
        kernel_fn,
        grid=(n // tm,),
        in_specs=in_specs,
        out_specs=[spec[lay] for lay, _ in outs],
        out_shape=[jax.ShapeDtypeStruct(shape[lay], dt) for lay, dt in outs],
        compiler_params=_params("parallel"),
        name=name,
    )(*args)


def _lambda(lq1_ref, lk1_ref, lq2_ref, lk2_ref, lam_init):
    a = jnp.sum(lq1_ref[...] * lk1_ref[...], axis=-1, keepdims=True)
    b = jnp.sum(lq2_ref[...] * lk2_ref[...], axis=-1, keepdims=True)
    return jnp.exp(a) - jnp.exp(b) + lam_init


def _attn_kernel(q1t_ref, q2t_ref, k_ref, vt_ref, g_ref, sw_ref, lq1_ref, lk1_ref, lq2_ref, lk2_ref,
                 o_ref, m_ref, l_ref, acc_ref, *, lam_init, tile):
    qi = pl.program_id(2)
    m_ref[...] = jnp.full(m_ref.shape, -jnp.inf, f32)
    l_ref[...] = jnp.zeros(l_ref.shape, f32)
    acc_ref[...] = jnp.zeros(acc_ref.shape, f32)

    def chunk(j, masked):
        kc = k_ref[pl.ds(pl.multiple_of(j * tile, tile), tile), :]
        vc = vt_ref[j]
        for c, qt_ref in enumerate((q1t_ref, q2t_ref)):
            s = _dot(kc, qt_ref[...])
            if masked:
                key = lax.broadcasted_iota(jnp.int32, s.shape, 0)
                qry = lax.broadcasted_iota(jnp.int32, s.shape, 1)
                s = jnp.where(key <= qry, s, -jnp.inf)
            m_prev = m_ref[c]
            m_new = jnp.maximum(m_prev, jnp.max(s, axis=0, keepdims=True))
            alpha = jnp.exp2(m_prev - m_new)
            p = jnp.exp2(s - m_new)
            l_ref[c] = alpha * l_ref[c] + jnp.sum(p, axis=0, keepdims=True)
            acc_ref[c] = alpha * acc_ref[c] + _dot(vc, p.astype(bf16))
            m_ref[c] = m_new

    def body(j, carry):
        chunk(j, False)
        return carry

    lax.fori_loop(0, qi, body, 0)
    chunk(qi, True)

    lam = _lambda(lq1_ref, lk1_ref, lq2_ref, lk2_ref, lam_init)
    ot = acc_ref[0] * (1.0 / l_ref[0]) - lam * (acc_ref[1] * (1.0 / l_ref[1]))
    ot = ot * lax.rsqrt(jnp.mean(ot * ot, axis=0, keepdims=True) + EPS)
    o = ot.T * sw_ref[...] * (1.0 - lam_init)
    o_ref[...] = (o * g_ref[...].astype(f32)).astype(bf16)


def _prompt_attention(q1t, q2t, kb, vt, g, sw, lams, lam_init, batch, seq, tile):
    nq = seq // tile
    qspec = pl.BlockSpec((None, HEAD_W, tile), lambda b, h, qi: (b * nq + qi, h, 0))
    kspec = pl.BlockSpec((seq, HEAD_W), lambda b, h, qi: (b, h))
    vspec = pl.BlockSpec((nq, HEAD_W, tile), lambda b, h, qi: (b, h, 0))
    ospec = pl.BlockSpec((tile, HEAD_W), lambda b, h, qi: (b * nq + qi, h))
    small = lambda w: pl.BlockSpec((1, w), lambda b, h, qi: (0, 0))
    return pl.pallas_call(
        functools.partial(_attn_kernel, lam_init=lam_init, tile=tile),
        grid=(batch, N_HEADS, nq),
        in_specs=[qspec, qspec, kspec, vspec, ospec, small(HEAD_W)] + [small(HEAD_DIM)] * 4,
        out_specs=ospec,
        out_shape=jax.ShapeDtypeStruct((batch * seq, D_MODEL), bf16),
        scratch_shapes=[pltpu.VMEM((2, 1, tile), f32), pltpu.VMEM((2, 1, tile), f32),
                        pltpu.VMEM((2, HEAD_W, tile), f32)],
        compiler_params=_params("parallel", "parallel", "parallel"),
        name="prompt_attention",
    )(q1t, q2t, kb, vt, g, sw, *lams)


def _decode_kernel(pt_ref, q_ref, kn_ref, vn_ref, g_ref, sw_ref, lq1_ref, lk1_ref, lq2_ref, lk2_ref,
                   *refs, lam_init, n_steps, group):
    kp_refs, vp_refs = refs[:group], refs[group:2 * group]
    o_ref, m_ref, l_ref, acc_ref = refs[2 * group:]
    step = pl.program_id(1)
    rows = PAGE_SIZE * N_HEADS

    @pl.when(step == 0)
    def _():
        m_ref[...] = jnp.full(m_ref.shape, -jnp.inf, f32)
        l_ref[...] = jnp.zeros(l_ref.shape, f32)
        acc_ref[...] = jnp.zeros(acc_ref.shape, f32)

    q = q_ref[...].astype(bf16)
    s = jnp.concatenate([_dot_nt(q, kp[...].reshape(rows, HEAD_W).astype(bf16)) for kp in kp_refs],
                        axis=1)
    r_head = lax.broadcasted_iota(jnp.int32, s.shape, 0) % N_HEADS
    c_head = lax.broadcasted_iota(jnp.int32, s.shape, 1) % N_HEADS
    s = jnp.where(r_head == c_head, s, -jnp.inf)
    m_prev = m_ref[...]
    m_new = jnp.maximum(m_prev, jnp.max(s, axis=-1, keepdims=True))
    alpha = jnp.exp2(m_prev - m_new)
    p = jnp.exp2(s - m_new)
    l_ref[...] = alpha * l_ref[...] + jnp.sum(p, axis=-1, keepdims=True)
    pb = p.astype(bf16)
    pv = _dot(pb[:, :rows], vp_refs[0][...].reshape(rows, HEAD_W).astype(bf16))
    for gi in range(1, group):
        pv = pv + _dot(pb[:, gi * rows:(gi + 1) * rows], vp_refs[gi][...].reshape(rows, HEAD_W).astype(bf16))
    acc_ref[...] = alpha * acc_ref[...] + pv
    m_ref[...] = m_new

    @pl.when(step == n_steps - 1)
    def _():
        kn = kn_ref[...]
        vn = vn_ref[...]
        s_n = jnp.sum(q_ref[...] * jnp.concatenate([kn, kn], axis=0), axis=-1, keepdims=True)
        m_prev = m_ref[...]
        m_new = jnp.maximum(m_prev, s_n)
        alpha = jnp.exp2(m_prev - m_new)
        p_n = jnp.exp2(s_n - m_new)
        l = alpha * l_ref[...] + p_n
        acc = (alpha * acc_ref[...] + p_n * jnp.concatenate([vn, vn], axis=0)) / l
        lam = _lambda(lq1_ref, lk1_ref, lq2_ref, lk2_ref, lam_init)
        o = acc[:N_HEADS] - lam * acc[N_HEADS:]
        o = _rms(o, sw_ref[...]) * (1.0 - lam_init)
        o_ref[...] = o * g_ref[...]


def _sample_attention(page_table, q, kn, vn, g, sw, lams, cache_k, cache_v, layer, lam_init):
    nb, n_pages = page_table.shape
    per_b = lambda r: pl.BlockSpec((None, r, HEAD_W), lambda b, p, pt: (b, 0, 0))
    small = lambda w: pl.BlockSpec((1, w), lambda b, p, pt: (0, 0))
    group = DECODE_PAGES_PER_STEP
    n_steps = n_pages // group

    def page(gi):
        return pl.BlockSpec((None, None, PAGE_SIZE, N_HEADS, HEAD_W),
                            lambda b, p, pt: (layer, pt[b, p * group + gi], 0, 0, 0))

    pages = [page(gi) for gi in range(group)]
    grid_spec = pltpu.PrefetchScalarGridSpec(
        num_scalar_prefetch=1,
        grid=(nb, n_steps),
        in_specs=[per_b(2 * N_HEADS), per_b(N_HEADS), per_b(N_HEADS), per_b(N_HEADS), small(HEAD_W)]
                 + [small(HEAD_DIM)] * 4 + pages + pages,
        out_specs=per_b(N_HEADS),
        scratch_shapes=[pltpu.VMEM((2 * N_HEADS, 1), f32), pltpu.VMEM((2 * N_HEADS, 1), f32),
                        pltpu.VMEM((2 * N_HEADS, HEAD_W), f32)],
    )
    return pl.pallas_call(
        functools.partial(_decode_kernel, lam_init=lam_init, n_steps=n_steps, group=group),
        grid_spec=grid_spec,
        out_shape=jax.ShapeDtypeStruct((nb, N_HEADS, HEAD_W), f32),
        compiler_params=_params("parallel", "arbitrary"),
        name="sample_attention",
    )(page_table, q, kn, vn, g, sw, *lams, *([cache_k] * group), *([cache_v] * group))


def _conv_tail(y, h_ref, wg_ref, g_ref, b_ref, o_ref):
    mu = jnp.mean(y, axis=-1, keepdims=True)
    yc = y - mu
    var = jnp.mean(yc * yc, axis=-1, keepdims=True)
    c = _silu(yc * lax.rsqrt(var + EPS) * g_ref[...] + b_ref[...])
    o_ref[...] = (c * _silu(_dot(h_ref[...], wg_ref[...]))).astype(bf16)


def _conv_kernel(u_ref, halo_ref, h_ref, wg_ref, cw_ref, cb_ref, g_ref, b_ref, o_ref,
                 full_ref, shift_ref, y_ref, *, tm, tiles_per_seq):
    i = pl.program_id(0)
    full_ref[0:CONV_HALO, :] = jnp.where(i % tiles_per_seq == 0, 0.0, halo_ref[...])
    full_ref[CONV_HALO:, :] = u_ref[...]
    span = tm + CONV_HALO - 8
    for s in range(1, 8):
        shift_ref[s - 1] = full_ref[s:s + span, :]

    def body(rb, carry):
        r0 = pl.multiple_of(rb * CONV_ROW_CHUNK, CONV_ROW_CHUNK)
        acc = jnp.broadcast_to(cb_ref[...], (CONV_ROW_CHUNK, D_MODEL))
        for k in range(CONV_K):
            off = CONV_HALO - CONV_STATE + k
            s, a = off % 8, off - off % 8
            if s == 0:
                tap = full_ref[pl.ds(r0 + a, CONV_ROW_CHUNK), :]
            else:
                tap = shift_ref[s - 1, pl.ds(r0 + a, CONV_ROW_CHUNK), :]
            acc = acc + cw_ref[k:k + 1, :] * tap
        y_ref[pl.ds(r0, CONV_ROW_CHUNK), :] = acc
        return carry

    lax.fori_loop(0, tm // CONV_ROW_CHUNK, body, 0)
    _conv_tail(y_ref[...], h_ref, wg_ref, g_ref, b_ref, o_ref)


def _prompt_conv(u, h, w_in_b, cw, cb, g, b, seq, tm):
    n = u.shape[0]
    ratio = tm // CONV_HALO
    halo = pl.BlockSpec((CONV_HALO, D_MODEL), lambda i: (jnp.maximum(i * ratio - 1, 0), 0))
    return pl.pallas_call(
        functools.partial(_conv_kernel, tm=tm, tiles_per_seq=seq // tm),
        grid=(n // tm,),
        in_specs=[_row_spec(tm), halo, _row_spec(tm), _const_spec((D_MODEL, D_MODEL), (0, COL_GCONV)),
                  _const_spec((CONV_K, D_MODEL))] + [_const_spec((1, D_MODEL))] * 3,
        out_specs=_row_spec(tm),
        out_shape=jax.ShapeDtypeStruct((n, D_MODEL), bf16),
        scratch_shapes=[pltpu.VMEM((tm + CONV_HALO, D_MODEL), f32),
                        pltpu.VMEM((7, tm + CONV_HALO - 8, D_MODEL), f32),
                        pltpu.VMEM((tm, D_MODEL), f32)],
        compiler_params=_params("parallel"),
        name="prompt_conv",
    )(u, u, h, w_in_b, cw, cb, g, b)


def _sample_conv_kernel(st_ref, u_ref, h_ref, wg_ref, cw_ref, cb_ref, g_ref, b_ref, o_ref):
    w = cw_ref[...]
    y = jnp.sum(st_ref[...] * w[None, :CONV_STATE, :], axis=1)
    y = y + u_ref[...] * w[CONV_STATE:, :] + cb_ref[...]
    _conv_tail(y, h_ref, wg_ref, g_ref, b_ref, o_ref)


def _sample_conv(state, u, h, w_in_b, cw, cb, g, b):
    nb = u.shape[0]
    return pl.pallas_call(
        _sample_conv_kernel,
        grid=(1,),
        in_specs=[pl.BlockSpec((nb, CONV_STATE, D_MODEL), lambda i: (0, 0, 0)), _row_spec(nb), _row_spec(nb),
                  _const_spec((D_MODEL, D_MODEL), (0, COL_GCONV)), _const_spec((CONV_K, D_MODEL))]
                 + [_const_spec((1, D_MODEL))] * 3,
        out_specs=_row_spec(nb),
        out_shape=jax.ShapeDtypeStruct((nb, D_MODEL), bf16),
        compiler_params=_params("arbitrary"),
        name="sample_conv",
    )(state, u, h, w_in_b, cw, cb, g, b)


def _merge_kernel(x_ref, h_ref, oa_ref, oc_ref, p_ref, wga_ref, wgb_ref, wap_ref, wcp_ref, wout_ref,
                  wple_ref, wpg_ref, npost_ref, npre_ref, xo_ref, ho_ref):
    h = h_ref[...]
    o_att = _dot(oa_ref[...], wap_ref[...])
    o_conv = _dot(oc_ref[...], wcp_ref[...])
    m = _sigmoid(_dot(h, wga_ref[...])) * o_att + _sigmoid(_dot(h, wgb_ref[...])) * o_conv
    x1 = x_ref[...] + _rms(_dot(m.astype(bf16), wout_ref[...]), npost_ref[...])
    ple = _dot(p_ref[...].astype(bf16), wple_ref[...])
    x2 = x1 + ple * _sigmoid(_dot(x1.astype(bf16), wpg_ref[...]))
    xo_ref[...] = x2
    ho_ref[...] = _rms(x2, npre_ref[...]).astype(bf16)


def _merge(x, h, oa, oc, p, w_in_b, wap, wcp, wout, wple, wpg, npost, npre_next, tm):
    n = x.shape[0]
    sq = _const_spec((D_MODEL, D_MODEL))
    return pl.pallas_call(
        _merge_kernel,
        grid=(n // tm,),
        in_specs=[_row_spec(tm)] * 4 + [_row_spec(tm, PLE_DIM),
                  _const_spec((D_MODEL, D_MODEL), (0, COL_GA)), _const_spec((D_MODEL, D_MODEL), (0, COL_GB)),
                  sq, sq, sq, _const_spec((PLE_DIM, D_MODEL)), sq,
                  _const_spec((1, D_MODEL)), _const_spec((1, D_MODEL))],
        out_specs=[_row_spec(tm), _row_spec(tm)],
        out_shape=[jax.ShapeDtypeStruct((n, D_MODEL), f32), jax.ShapeDtypeStruct((n, D_MODEL), bf16)],
        compiler_params=_params("parallel"),
        name="merge",
    )(x, h, oa, oc, p, w_in_b, w_in_b, wap, wcp, wout, wple, wpg, npost, npre_next)


def _rope_tables(pos, rows):
    inv = 1.0 / (ROPE_THETA ** (jnp.arange(0, HEAD_DIM, 2, dtype=f32) / HEAD_DIM))
    ang = pos.astype(f32)[:, None] * inv[None, :]
    ang = jnp.concatenate([ang, ang], axis=-1)
    reps = HEAD_W // HEAD_DIM
    cos = jnp.tile(jnp.cos(ang), (1, reps))
    sin = jnp.tile(jnp.sin(ang), (1, reps))
    first = (jnp.arange(HEAD_W) % HEAD_DIM) < HEAD_DIM // 2
    tables = (cos, jnp.where(first, -sin, 0.0), jnp.where(first, 0.0, sin))
    return tuple(jnp.broadcast_to(t, (rows, HEAD_W)) for t in tables)


def _trunk(x, p, tables, attn_fn, conv_fn, wts, tm, tm_attn, prompt):
    ks, vs, us = [], [], []
    h = _prenorm(x, wts["norm_pre"][0][None], tm)
    for l in range(DEPTH):
        w_in_b = wts["w_in"][l]
        lam_init = 0.8 - 0.6 * math.exp(-0.3 * l)
        q_lay = "cols" if prompt else "rows"
        q1, q2 = _project(functools.partial(_q_kernel, transposed=prompt), "proj_q", h, w_in_b, COL_Q, tm_attn,
                          ((q_lay, bf16), (q_lay, bf16)), tables)
        k, kb = _project(_k_kernel, "proj_k", h, w_in_b, COL_K, tm_attn, (("heads", f32), ("rows", bf16)), tables)
        if prompt:
            v, vt = _project(_v_kernel, "proj_v", h, w_in_b, COL_V, tm_attn, (("heads", f32), ("cols", bf16)))
        else:
            (v,), vt = _project(_v_rows_kernel, "proj_v", h, w_in_b, COL_V, tm_attn, (("heads", f32),)), None
        (g_att,) = _project(_silu_kernel, "proj_gatt", h, w_in_b, COL_GATT, tm, (("rows", bf16),))
        (u,) = _project(_glu_kernel, "proj_glu", h, w_in_b, COL_UA, tm, (("rows", f32),), wcols=2 * D_MODEL)
        lams = tuple(wts[n][l][None] for n in ("lambda_q1", "lambda_k1", "lambda_q2", "lambda_k2"))
        oa = attn_fn(l, q1, q2, k, v, kb, vt, g_att, wts["subln_w"][l][None], lams, lam_init)
        oc = conv_fn(l, u, h, w_in_b, wts["conv_w"][l], wts["conv_b"][l][None],
                     wts["conv_ln_g"][l][None], wts["conv_ln_b"][l][None])
        x, h = _merge(x, h, oa, oc, p[l], w_in_b, wts["w_attn_proj"][l], wts["w_conv_proj"][l],
                      wts["w_out"][l], wts["w_ple"][l], wts["w_ple_gate"][l],
                      wts["norm_post"][l][None], wts["norm_pre"][(l + 1) % DEPTH][None], tm)
        ks.append(k)
        vs.append(v)
        us.append(u)
    return x, jnp.stack(ks), jnp.stack(vs), us


def kernel(x_prompt, x_sample, cache_k, cache_v, state_conv, page_table, p_prompt, p_sample,
           norm_pre, norm_post, w_in, lambda_q1, lambda_k1, lambda_q2, lambda_k2, subln_w,
           w_attn_proj, conv_w, conv_b, conv_ln_g, conv_ln_b, w_conv_proj, w_out, w_ple, w_ple_gate):
    batch, seq, _ = x_prompt.shape
    nb, dec_seq, _ = x_sample.shape
    assert dec_seq == 1
    n_pages = page_table.shape[1]
    past = n_pages * PAGE_SIZE
    wts = dict(norm_pre=norm_pre, norm_post=norm_post, w_in=w_in.astype(bf16),
               lambda_q1=lambda_q1, lambda_k1=lambda_k1, lambda_q2=lambda_q2, lambda_k2=lambda_k2,
               subln_w=subln_w, w_attn_proj=w_attn_proj.astype(bf16), conv_w=conv_w, conv_b=conv_b,
               conv_ln_g=conv_ln_g, conv_ln_b=conv_ln_b, w_conv_proj=w_conv_proj.astype(bf16),
               w_out=w_out.astype(bf16), w_ple=w_ple.astype(bf16), w_ple_gate=w_ple_gate.astype(bf16))

    assert seq % ATTN_TILE == 0 and ATTN_TILE % ROW_TILE == 0 and n_pages % DECODE_PAGES_PER_STEP == 0

    def attn_prompt(l, q1t, q2t, k, v, kb, vt, g_att, sw, lams, lam_init):
        return _prompt_attention(q1t, q2t, kb, vt, g_att, sw, lams, lam_init, batch, seq, ATTN_TILE)

    def conv_prompt_fn(l, u, h, w_in_b, cw, cb, g, b):
        return _prompt_conv(u, h, w_in_b, cw, cb, g, b, seq, ROW_TILE)

    tables_p = _rope_tables(jnp.arange(seq, dtype=jnp.int32), seq)
    y_p, k_p, v_p, u_p = _trunk(x_prompt.reshape(batch * seq, D_MODEL),
                                p_prompt.reshape(DEPTH, batch * seq, PLE_DIM),
                                tables_p, attn_prompt, conv_prompt_fn, wts, ROW_TILE, ATTN_TILE, True)
    conv_p = jnp.stack([u.reshape(batch, seq, D_MODEL)[:, seq - CONV_STATE:] for u in u_p])

    def attn_sample(l, q1, q2, k, v, kb, vt, g_att, sw, lams, lam_init):
        per_head = lambda a: a.astype(f32).reshape(nb, N_HEADS, HEAD_W)
        q = jnp.concatenate([per_head(q1), per_head(q2)], axis=1)
        o = _sample_attention(page_table, q, k, v, per_head(g_att), sw, lams, cache_k, cache_v, l, lam_init)
        return o.reshape(nb, D_MODEL).astype(bf16)

    def conv_sample_fn(l, u, h, w_in_b, cw, cb, g, b):
        return _sample_conv(state_conv[l], u, h, w_in_b, cw, cb, g, b)

    tables_s = _rope_tables(past + jnp.arange(1, dtype=jnp.int32), nb)
    y_s, k_s, v_s, u_s = _trunk(x_sample.reshape(nb, D_MODEL), p_sample.reshape(DEPTH, nb, PLE_DIM),
                                tables_s, attn_sample, conv_sample_fn, wts, nb, nb, False)
    conv_s = jnp.stack([jnp.concatenate([state_conv[l][:, 1:], u_s[l][:, None]], axis=1) for l in range(DEPTH)])

    hshape = (N_HEADS, HEAD_W)
    return (y_p.reshape(batch, seq, D_MODEL), y_s.reshape(nb, 1, D_MODEL),
            k_p.reshape(DEPTH, batch, seq, *hshape), v_p.reshape(DEPTH, batch, seq, *hshape), conv_p,
            k_s.reshape(DEPTH, nb, 1, *hshape), v_s.reshape(DEPTH, nb, 1, *hshape), conv_s)
```

```python
import functools
import math

import jax
import jax.numpy as jnp
from jax import lax
from jax.experimental import pallas as pl
from jax.experimental.pallas import tpu as pltpu

D_MODEL = 1024
N_HEADS = 8
HEAD_DIM = 64
HEAD_W = 2 * HEAD_DIM
CONV_K = 31
CONV_STATE = CONV_K - 1
PLE_DIM = 256
PAGE_SIZE = 128
ROPE_THETA = 10000.0
EPS = 1e-6
DEPTH = 4

COL_Q, COL_K, COL_V, COL_GATT, COL_UA, COL_GCONV, COL_GA, COL_GB = 0, 1, 2, 3, 4, 6, 7, 8

Q_SCALE = HEAD_DIM ** -0.5 * math.log2(math.e)
ATTN_TILE = 512
ROW_TILE = 256
DECODE_PAGES_PER_STEP = 8

VMEM_LIMIT_BYTES = 56 * 1024 * 1024
CONV_HALO = 32
CONV_ROW_CHUNK = 32

f32 = jnp.float32
bf16 = jnp.bfloat16


def _params(*sem):
    return pltpu.CompilerParams(dimension_semantics=sem, vmem_limit_bytes=VMEM_LIMIT_BYTES)


def _dot(a, b):
    return jnp.dot(a, b, preferred_element_type=f32)


def _dot_nt(a, b):
    return lax.dot_general(a, b, (((1,), (1,)), ((), ())), preferred_element_type=f32)


def _sigmoid(z):
    return 1.0 / (1.0 + jnp.exp(-z))


def _silu(z):
    return z * _sigmoid(z)


def _rms(xf, w):
    return xf * lax.rsqrt(jnp.mean(xf * xf, axis=-1, keepdims=True) + EPS) * w


def _row_spec(tm, width=D_MODEL):
    return pl.BlockSpec((tm, width), lambda i: (i, 0))


def _const_spec(shape, idx=(0, 0)):
    return pl.BlockSpec(shape, lambda i: idx)


def _norm_kernel(x_ref, w_ref, h_ref):
    h_ref[...] = _rms(x_ref[...], w_ref[...]).astype(bf16)


def _prenorm(x, w, tm):
    n = x.shape[0]
    return pl.pallas_call(
        _norm_kernel,
        grid=(n // tm,),
        in_specs=[_row_spec(tm), _const_spec((1, D_MODEL))],
        out_specs=_row_spec(tm),
        out_shape=jax.ShapeDtypeStruct((n, D_MODEL), bf16),
        compiler_params=_params("parallel"),
        name="prenorm",
    )(x, w)


def _rope(z, cos_ref, sa_ref, sb_ref):
    wide = lambda t_ref: jnp.concatenate([t_ref[...]] * N_HEADS, axis=1)
    return (z * wide(cos_ref)
            + pltpu.roll(z, D_MODEL - HEAD_DIM // 2, 1) * wide(sa_ref)
            + pltpu.roll(z, HEAD_DIM // 2, 1) * wide(sb_ref))


def _q_kernel(h_ref, w_ref, cos_ref, sa_ref, sb_ref, q1_ref, q2_ref, *, transposed):
    r = _rope(_dot(h_ref[...], w_ref[...]), cos_ref, sa_ref, sb_ref) * Q_SCALE
    if transposed:
        r = r.T
    idx = lax.broadcasted_iota(jnp.int32, r.shape, 0 if transposed else 1)
    first = (idx & HEAD_DIM) == 0
    q1_ref[...] = jnp.where(first, r, 0.0).astype(bf16)
    q2_ref[...] = jnp.where(first, 0.0, r).astype(bf16)


def _store_heads(o_ref, z):
    for hd in range(N_HEADS):
        o_ref[:, hd, :] = z[:, hd * HEAD_W:(hd + 1) * HEAD_W]


def _k_kernel(h_ref, w_ref, cos_ref, sa_ref, sb_ref, k_ref, kb_ref):
    r = _rope(_dot(h_ref[...], w_ref[...]), cos_ref, sa_ref, sb_ref)
    _store_heads(k_ref, r)
    kb_ref[...] = r.astype(bf16)


def _v_kernel(h_ref, w_ref, v_ref, vt_ref):
    z = _dot(h_ref[...], w_ref[...])
    _store_heads(v_ref, z)
    vt_ref[...] = z.T.astype(bf16)


def _v_rows_kernel(h_ref, w_ref, v_ref):
    _store_heads(v_ref, _dot(h_ref[...], w_ref[...]))


def _silu_kernel(h_ref, w_ref, o_ref):
    o_ref[...] = _silu(_dot(h_ref[...], w_ref[...])).astype(bf16)


def _glu_kernel(h_ref, w_ref, u_ref):
    z = _dot(h_ref[...], w_ref[...])
    u_ref[...] = z[:, :D_MODEL] * _sigmoid(z[:, D_MODEL:])


def _project(kernel_fn, name, h, w_in_b, col, tm, outs, tables=None, wcols=D_MODEL):
    n = h.shape[0]
    in_specs = [_row_spec(tm), _const_spec((D_MODEL, wcols), (0, col * D_MODEL // wcols))]
    args = [h, w_in_b]
    if tables is not None:
        tb = tables[0].shape[0] // tm
        in_specs += [pl.BlockSpec((tm, HEAD_W), lambda i: (i % tb, 0))] * 3
        args += list(tables)
    spec = dict(rows=_row_spec(tm),
                heads=pl.BlockSpec((tm, N_HEADS, HEAD_W), lambda i: (i, 0, 0)),
                cols=pl.BlockSpec((None, D_MODEL, tm), lambda i: (i, 0, 0)))
    shape = dict(rows=(n, D_MODEL), heads=(n, N_HEADS, HEAD_W), cols=(n // tm, D_MODEL, tm))
    return pl.pallas_call(
        kernel_fn,
        grid=(n // tm,),
        in_specs=in_specs,
        out_specs=[spec[lay] for lay, _ in outs],
        out_shape=[jax.ShapeDtypeStruct(shape[lay], dt) for lay, dt in outs],
        compiler_params=_params("parallel"),
        name=name,
    )(*args)


def _lambda(lq1_ref, lk1_ref, lq2_ref, lk2_ref, lam_init):
    a = jnp.sum(lq1_ref[...] * lk1_ref[...], axis=-1, keepdims=True)
    b = jnp.sum(lq2_ref[...] * lk2_ref[...], axis=-1, keepdims=True)
    return jnp.exp(a) - jnp.exp(b) + lam_init


def _attn_kernel(q1t_ref, q2t_ref, k_ref, vt_ref, g_ref, sw_ref, lq1_ref, lk1_ref, lq2_ref, lk2_ref,
                 o_ref, m_ref, l_ref, acc_ref, sa_ref, sb_ref, *, lam_init, tile):
    qi = pl.program_id(2)
    m_ref[...] = jnp.full(m_ref.shape, -jnp.inf, f32)
    l_ref[...] = jnp.zeros(l_ref.shape, f32)
    acc_ref[...] = jnp.zeros(acc_ref.shape, f32)

    def scores(j, s_ref):
        kc = k_ref[pl.ds(pl.multiple_of(j * tile, tile), tile), :]
        s_ref[0] = _dot(kc, q1t_ref[...])
        s_ref[1] = _dot(kc, q2t_ref[...])

    def update(j, s_ref, masked):
        vc = vt_ref[j]
        for c in range(2):
            s = s_ref[c]
            if masked:
                key = lax.broadcasted_iota(jnp.int32, s.shape, 0)
                qry = lax.broadcasted_iota(jnp.int32, s.shape, 1)
                s = jnp.where(key <= qry, s, -jnp.inf)
            m_prev = m_ref[c]
            m_new = jnp.maximum(m_prev, jnp.max(s, axis=0, keepdims=True))
            alpha = jnp.exp2(m_prev - m_new)
            p = jnp.exp2(s - m_new)
            l_ref[c] = alpha * l_ref[c] + jnp.sum(p, axis=0, keepdims=True)
            acc_ref[c] = alpha * acc_ref[c] + _dot(vc, p.astype(bf16))
            m_ref[c] = m_new

    def pair(t, carry):
        j = 2 * t
        scores(j + 1, sb_ref)
        update(j, sa_ref, False)
        scores(j + 2, sa_ref)
        update(j + 1, sb_ref, False)
        return carry

    scores(0, sa_ref)
    lax.fori_loop(0, lax.shift_right_logical(qi, 1), pair, 0)

    @pl.when((qi & 1) == 0)
    def _():
        update(qi, sa_ref, True)

    @pl.when((qi & 1) == 1)
    def _():
        scores(qi, sb_ref)
        update(qi - 1, sa_ref, False)
        update(qi, sb_ref, True)

    lam = _lambda(lq1_ref, lk1_ref, lq2_ref, lk2_ref, lam_init)
    ot = acc_ref[0] * (1.0 / l_ref[0]) - lam * (acc_ref[1] * (1.0 / l_ref[1]))
    ot = ot * lax.rsqrt(jnp.mean(ot * ot, axis=0, keepdims=True) + EPS)
    o = ot.T * sw_ref[...] * (1.0 - lam_init)
    o_ref[...] = (o * g_ref[...].astype(f32)).astype(bf16)


def _prompt_attention(q1t, q2t, kb, vt, g, sw, lams, lam_init, batch, seq, tile):
    nq = seq // tile
    qspec = pl.BlockSpec((None, HEAD_W, tile), lambda b, h, qi: (b * nq + qi, h, 0))
    kspec = pl.BlockSpec((seq, HEAD_W), lambda b, h, qi: (b, h))
    vspec = pl.BlockSpec((nq, HEAD_W, tile), lambda b, h, qi: (b, h, 0))
    ospec = pl.BlockSpec((tile, HEAD_W), lambda b, h, qi: (b * nq + qi, h))
    small = lambda w: pl.BlockSpec((1, w), lambda b, h, qi: (0, 0))
    return pl.pallas_call(
        functools.partial(_attn_kernel, lam_init=lam_init, tile=tile),
        grid=(batch, N_HEADS, nq),
        in_specs=[qspec, qspec, kspec, vspec, ospec, small(HEAD_W)] + [small(HEAD_DIM)] * 4,
        out_specs=ospec,
        out_shape=jax.ShapeDtypeStruct((batch * seq, D_MODEL), bf16),
        scratch_shapes=[pltpu.VMEM((2, 1, tile), f32), pltpu.VMEM((2, 1, tile), f32),
                        pltpu.VMEM((2, HEAD_W, tile), f32),
                        pltpu.VMEM((2, tile, tile), f32), pltpu.VMEM((2, tile, tile), f32)],
        compiler_params=_params("parallel", "parallel", "parallel"),
        name="prompt_attention",
    )(q1t, q2t, kb, vt, g, sw, *lams)


def _decode_kernel(pt_ref, q_ref, kn_ref, vn_ref, g_ref, sw_ref, lq1_ref, lk1_ref, lq2_ref, lk2_ref,
                   *refs, lam_init, n_steps, group):
    kp_refs, vp_refs = refs[:group], refs[group:2 * group]
    o_ref, m_ref, l_ref, acc_ref = refs[2 * group:]
    step = pl.program_id(1)
    rows = PAGE_SIZE * N_HEADS

    @pl.when(step == 0)
    def _():
        m_ref[...] = jnp.full(m_ref.shape, -jnp.inf, f32)
        l_ref[...] = jnp.zeros(l_ref.shape, f32)
        acc_ref[...] = jnp.zeros(acc_ref.shape, f32)

    q = q_ref[...].astype(bf16)
    s = jnp.concatenate([_dot_nt(q, kp[...].reshape(rows, HEAD_W).astype(bf16)) for kp in kp_refs],
                        axis=1)
    r_head = lax.broadcasted_iota(jnp.int32, s.shape, 0) % N_HEADS
    c_head = lax.broadcasted_iota(jnp.int32, s.shape, 1) % N_HEADS
    s = jnp.where(r_head == c_head, s, -jnp.inf)
    m_prev = m_ref[...]
    m_new = jnp.maximum(m_prev, jnp.max(s, axis=-1, keepdims=True))
    alpha = jnp.exp2(m_prev - m_new)
    p = jnp.exp2(s - m_new)
    l_ref[...] = alpha * l_ref[...] + jnp.sum(p, axis=-1, keepdims=True)
    pb = p.astype(bf16)
    pv = _dot(pb[:, :rows], vp_refs[0][...].reshape(rows, HEAD_W).astype(bf16))
    for gi in range(1, group):
        pv = pv + _dot(pb[:, gi * rows:(gi + 1) * rows], vp_refs[gi][...].reshape(rows, HEAD_W).astype(bf16))
    acc_ref[...] = alpha * acc_ref[...] + pv
    m_ref[...] = m_new

    @pl.when(step == n_steps - 1)
    def _():
        kn = kn_ref[...]
        vn = vn_ref[...]
        s_n = jnp.sum(q_ref[...] * jnp.concatenate([kn, kn], axis=0), axis=-1, keepdims=True)
        m_prev = m_ref[...]
        m_new = jnp.maximum(m_prev, s_n)
        alpha = jnp.exp2(m_prev - m_new)
        p_n = jnp.exp2(s_n - m_new)
        l = alpha * l_ref[...] + p_n
        acc = (alpha * acc_ref[...] + p_n * jnp.concatenate([vn, vn], axis=0)) / l
        lam = _lambda(lq1_ref, lk1_ref, lq2_ref, lk2_ref, lam_init)
        o = acc[:N_HEADS] - lam * acc[N_HEADS:]
        o = _rms(o, sw_ref[...]) * (1.0 - lam_init)
        o_ref[...] = o * g_ref[...]


def _sample_attention(page_table, q, kn, vn, g, sw, lams, cache_k, cache_v, layer, lam_init):
    nb, n_pages = page_table.shape
    per_b = lambda r: pl.BlockSpec((None, r, HEAD_W), lambda b, p, pt: (b, 0, 0))
    small = lambda w: pl.BlockSpec((1, w), lambda b, p, pt: (0, 0))
    group = DECODE_PAGES_PER_STEP
    n_steps = n_pages // group

    def page(gi):
        return pl.BlockSpec((None, None, PAGE_SIZE, N_HEADS, HEAD_W),
                            lambda b, p, pt: (layer, pt[b, p * group + gi], 0, 0, 0))

    pages = [page(gi) for gi in range(group)]
    grid_spec = pltpu.PrefetchScalarGridSpec(
        num_scalar_prefetch=1,
        grid=(nb, n_steps),
        in_specs=[per_b(2 * N_HEADS), per_b(N_HEADS), per_b(N_HEADS), per_b(N_HEADS), small(HEAD_W)]
                 + [small(HEAD_DIM)] * 4 + pages + pages,
        out_specs=per_b(N_HEADS),
        scratch_shapes=[pltpu.VMEM((2 * N_HEADS, 1), f32), pltpu.VMEM((2 * N_HEADS, 1), f32),
                        pltpu.VMEM((2 * N_HEADS, HEAD_W), f32)],
    )
    return pl.pallas_call(
        functools.partial(_decode_kernel, lam_init=lam_init, n_steps=n_steps, group=group),
        grid_spec=grid_spec,
        out_shape=jax.ShapeDtypeStruct((nb, N_HEADS, HEAD_W), f32),
        compiler_params=_params("parallel", "arbitrary"),
        name="sample_attention",
    )(page_table, q, kn, vn, g, sw, *lams, *([cache_k] * group), *([cache_v] * group))


def _conv_tail(y, h_ref, wg_ref, g_ref, b_ref, o_ref):
    mu = jnp.mean(y, axis=-1, keepdims=True)
    yc = y - mu
    var = jnp.mean(yc * yc, axis=-1, keepdims=True)
    c = _silu(yc * lax.rsqrt(var + EPS) * g_ref[...] + b_ref[...])
    o_ref[...] = (c * _silu(_dot(h_ref[...], wg_ref[...]))).astype(bf16)


def _conv_kernel(u_ref, halo_ref, h_ref, wg_ref, cw_ref, cb_ref, g_ref, b_ref, o_ref,
                 full_ref, shift_ref, y_ref, *, tm, tiles_per_seq):
    i = pl.program_id(0)
    full_ref[0:CONV_HALO, :] = jnp.where(i % tiles_per_seq == 0, 0.0, halo_ref[...])
    full_ref[CONV_HALO:, :] = u_ref[...]
    span = tm + CONV_HALO - 8
    for s in range(1, 8):
        shift_ref[s - 1] = full_ref[s:s + span, :]

    def body(rb, carry):
        r0 = pl.multiple_of(rb * CONV_ROW_CHUNK, CONV_ROW_CHUNK)
        acc = jnp.broadcast_to(cb_ref[...], (CONV_ROW_CHUNK, D_MODEL))
        for k in range(CONV_K):
            off = CONV_HALO - CONV_STATE + k
            s, a = off % 8, off - off % 8
            if s == 0:
                tap = full_ref[pl.ds(r0 + a, CONV_ROW_CHUNK), :]
            else:
                tap = shift_ref[s - 1, pl.ds(r0 + a, CONV_ROW_CHUNK), :]
            acc = acc + cw_ref[k:k + 1, :] * tap
        y_ref[pl.ds(r0, CONV_ROW_CHUNK), :] = acc
        return carry

    lax.fori_loop(0, tm // CONV_ROW_CHUNK, body, 0)
    _conv_tail(y_ref[...], h_ref, wg_ref, g_ref, b_ref, o_ref)


def _prompt_conv(u, h, w_in_b, cw, cb, g, b, seq, tm):
    n = u.shape[0]
    ratio = tm // CONV_HALO
    halo = pl.BlockSpec((CONV_HALO, D_MODEL), lambda i: (jnp.maximum(i * ratio - 1, 0), 0))
    return pl.pallas_call(
        functools.partial(_conv_kernel, tm=tm, tiles_per_seq=seq // tm),
        grid=(n // tm,),
        in_specs=[_row_spec(tm), halo, _row_spec(tm), _const_spec((D_MODEL, D_MODEL), (0, COL_GCONV)),
                  _const_spec((CONV_K, D_MODEL))] + [_const_spec((1, D_MODEL))] * 3,
        out_specs=_row_spec(tm),
        out_shape=jax.ShapeDtypeStruct((n, D_MODEL), bf16),
        scratch_shapes=[pltpu.VMEM((tm + CONV_HALO, D_MODEL), f32),
                        pltpu.VMEM((7, tm + CONV_HALO - 8, D_MODEL), f32),
                        pltpu.VMEM((tm, D_MODEL), f32)],
        compiler_params=_params("parallel"),
        name="prompt_conv",
    )(u, u, h, w_in_b, cw, cb, g, b)


def _sample_conv_kernel(st_ref, u_ref, h_ref, wg_ref, cw_ref, cb_ref, g_ref, b_ref, o_ref):
    w = cw_ref[...]
    y = jnp.sum(st_ref[...] * w[None, :CONV_STATE, :], axis=1)
    y = y + u_ref[...] * w[CONV_STATE:, :] + cb_ref[...]
    _conv_tail(y, h_ref, wg_ref, g_ref, b_ref, o_ref)


def _sample_conv(state, u, h, w_in_b, cw, cb, g, b):
    nb = u.shape[0]
    return pl.pallas_call(
        _sample_conv_kernel,
        grid=(1,),
        in_specs=[pl.BlockSpec((nb, CONV_STATE, D_MODEL), lambda i: (0, 0, 0)), _row_spec(nb), _row_spec(nb),
                  _const_spec((D_MODEL, D_MODEL), (0, COL_GCONV)), _const_spec((CONV_K, D_MODEL))]
                 + [_const_spec((1, D_MODEL))] * 3,
        out_specs=_row_spec(nb),
        out_shape=jax.ShapeDtypeStruct((nb, D_MODEL), bf16),
        compiler_params=_params("arbitrary"),
        name="sample_conv",
    )(state, u, h, w_in_b, cw, cb, g, b)


def _merge_kernel(x_ref, h_ref, oa_ref, oc_ref, p_ref, wga_ref, wgb_ref, wap_ref, wcp_ref, wout_ref,
                  wple_ref, wpg_ref, npost_ref, npre_ref, xo_ref, ho_ref):
    h = h_ref[...]
    o_att = _dot(oa_ref[...], wap_ref[...])
    o_conv = _dot(oc_ref[...], wcp_ref[...])
    m = _sigmoid(_dot(h, wga_ref[...])) * o_att + _sigmoid(_dot(h, wgb_ref[...])) * o_conv
    x1 = x_ref[...] + _rms(_dot(m.astype(bf16), wout_ref[...]), npost_ref[...])
    ple = _dot(p_ref[...].astype(bf16), wple_ref[...])
    x2 = x1 + ple * _sigmoid(_dot(x1.astype(bf16), wpg_ref[...]))
    xo_ref[...] = x2
    ho_ref[...] = _rms(x2, npre_ref[...]).astype(bf16)


def _merge(x, h, oa, oc, p, w_in_b, wap, wcp, wout, wple, wpg, npost, npre_next, tm):
    n = x.shape[0]
    sq = _const_spec((D_MODEL, D_MODEL))
    return pl.pallas_call(
        _merge_kernel,
        grid=(n // tm,),
        in_specs=[_row_spec(tm)] * 4 + [_row_spec(tm, PLE_DIM),
                  _const_spec((D_MODEL, D_MODEL), (0, COL_GA)), _const_spec((D_MODEL, D_MODEL), (0, COL_GB)),
                  sq, sq, sq, _const_spec((PLE_DIM, D_MODEL)), sq,
                  _const_spec((1, D_MODEL)), _const_spec((1, D_MODEL))],
        out_specs=[_row_spec(tm), _row_spec(tm)],
        out_shape=[jax.ShapeDtypeStruct((n, D_MODEL), f32), jax.ShapeDtypeStruct((n, D_MODEL), bf16)],
        compiler_params=_params("parallel"),
        name="merge",
    )(x, h, oa, oc, p, w_in_b, w_in_b, wap, wcp, wout, wple, wpg, npost, npre_next)


def _rope_tables(pos, rows):
    inv = 1.0 / (ROPE_THETA ** (jnp.arange(0, HEAD_DIM, 2, dtype=f32) / HEAD_DIM))
    ang = pos.astype(f32)[:, None] * inv[None, :]
    ang = jnp.concatenate([ang, ang], axis=-1)
    reps = HEAD_W // HEAD_DIM
    cos = jnp.tile(jnp.cos(ang), (1, reps))
    sin = jnp.tile(jnp.sin(ang), (1, reps))
    first = (jnp.arange(HEAD_W) % HEAD_DIM) < HEAD_DIM // 2
    tables = (cos, jnp.where(first, -sin, 0.0), jnp.where(first, 0.0, sin))
    return tuple(jnp.broadcast_to(t, (rows, HEAD_W)) for t in tables)


def _trunk(x, p, tables, attn_fn, conv_fn, wts, tm, tm_attn, prompt):
    ks, vs, us = [], [], []
    h = _prenorm(x, wts["norm_pre"][0][None], tm)
    for l in range(DEPTH):
        w_in_b = wts["w_in"][l]
        lam_init = 0.8 - 0.6 * math.exp(-0.3 * l)
        q_lay = "cols" if prompt else "rows"
        q1, q2 = _project(functools.partial(_q_kernel, transposed=prompt), "proj_q", h, w_in_b, COL_Q, tm_attn,
                          ((q_lay, bf16), (q_lay, bf16)), tables)
        k, kb = _project(_k_kernel, "proj_k", h, w_in_b, COL_K, tm_attn, (("heads", f32), ("rows", bf16)), tables)
        if prompt:
            v, vt = _project(_v_kernel, "proj_v", h, w_in_b, COL_V, tm_attn, (("heads", f32), ("cols", bf16)))
        else:
            (v,), vt = _project(_v_rows_kernel, "proj_v", h, w_in_b, COL_V, tm_attn, (("heads", f32),)), None
        (g_att,) = _project(_silu_kernel, "proj_gatt", h, w_in_b, COL_GATT, tm_attn, (("rows", bf16),))
        (u,) = _project(_glu_kernel, "proj_glu", h, w_in_b, COL_UA, tm_attn, (("rows", f32),), wcols=2 * D_MODEL)
        lams = tuple(wts[n][l][None] for n in ("lambda_q1", "lambda_k1", "lambda_q2", "lambda_k2"))
        oa = attn_fn(l, q1, q2, k, v, kb, vt, g_att, wts["subln_w"][l][None], lams, lam_init)
        oc = conv_fn(l, u, h, w_in_b, wts["conv_w"][l], wts["conv_b"][l][None],
                     wts["conv_ln_g"][l][None], wts["conv_ln_b"][l][None])
        x, h = _merge(x, h, oa, oc, p[l], w_in_b, wts["w_attn_proj"][l], wts["w_conv_proj"][l],
                      wts["w_out"][l], wts["w_ple"][l], wts["w_ple_gate"][l],
                      wts["norm_post"][l][None], wts["norm_pre"][(l + 1) % DEPTH][None], tm)
        ks.append(k)
        vs.append(v)
        us.append(u)
    return x, jnp.stack(ks), jnp.stack(vs), us


def kernel(x_prompt, x_sample, cache_k, cache_v, state_conv, page_table, p_prompt, p_sample,
           norm_pre, norm_post, w_in, lambda_q1, lambda_k1, lambda_q2, lambda_k2, subln_w,
           w_attn_proj, conv_w, conv_b, conv_ln_g, conv_ln_b, w_conv_proj, w_out, w_ple, w_ple_gate):
    batch, seq, _ = x_prompt.shape
    nb, dec_seq, _ = x_sample.shape
    assert dec_seq == 1
    n_pages = page_table.shape[1]
    past = n_pages * PAGE_SIZE
    wts = dict(norm_pre=norm_pre, norm_post=norm_post, w_in=w_in.astype(bf16),
               lambda_q1=lambda_q1, lambda_k1=lambda_k1, lambda_q2=lambda_q2, lambda_k2=lambda_k2,
               subln_w=subln_w, w_attn_proj=w_attn_proj.astype(bf16), conv_w=conv_w, conv_b=conv_b,
               conv_ln_g=conv_ln_g, conv_ln_b=conv_ln_b, w_conv_proj=w_conv_proj.astype(bf16),
               w_out=w_out.astype(bf16), w_ple=w_ple.astype(bf16), w_ple_gate=w_ple_gate.astype(bf16))

    assert seq % ATTN_TILE == 0 and ATTN_TILE % ROW_TILE == 0 and n_pages % DECODE_PAGES_PER_STEP == 0

    def attn_prompt(l, q1t, q2t, k, v, kb, vt, g_att, sw, lams, lam_init):
        return _prompt_attention(q1t, q2t, kb, vt, g_att, sw, lams, lam_init, batch, seq, ATTN_TILE)

    def conv_prompt_fn(l, u, h, w_in_b, cw, cb, g, b):
        return _prompt_conv(u, h, w_in_b, cw, cb, g, b, seq, ROW_TILE)

    tables_p = _rope_tables(jnp.arange(seq, dtype=jnp.int32), seq)
    y_p, k_p, v_p, u_p = _trunk(x_prompt.reshape(batch * seq, D_MODEL),
                                p_prompt.reshape(DEPTH, batch * seq, PLE_DIM),
                                tables_p, attn_prompt, conv_prompt_fn, wts, ROW_TILE, ATTN_TILE, True)
    conv_p = jnp.stack([u.reshape(batch, seq, D_MODEL)[:, seq - CONV_STATE:] for u in u_p])

    def attn_sample(l, q1, q2, k, v, kb, vt, g_att, sw, lams, lam_init):
        per_head = lambda a: a.astype(f32).reshape(nb, N_HEADS, HEAD_W)
        q = jnp.concatenate([per_head(q1), per_head(q2)], axis=1)
        o = _sample_attention(page_table, q, k, v, per_head(g_att), sw, lams, cache_k, cache_v, l, lam_init)
        return o.reshape(nb, D_MODEL).astype(bf16)

    def conv_sample_fn(l, u, h, w_in_b, cw, cb, g, b):
        return _sample_conv(state_conv[l], u, h, w_in_b, cw, cb, g, b)

    tables_s = _rope_tables(past + jnp.arange(1, dtype=jnp.int32), nb)
    y_s, k_s, v_s, u_s = _trunk(x_sample.reshape(nb, D_MODEL), p_sample.reshape(DEPTH, nb, PLE_DIM),
                                tables_s, attn_sample, conv_sample_fn, wts, nb, nb, False)
    conv_s = jnp.stack([jnp.concatenate([state_conv[l][:, 1:], u_s[l][:, None]], axis=1) for l in range(DEPTH)])

    hshape = (N_HEADS, HEAD_W)
    return (y_p.reshape(batch, seq, D_MODEL), y_s.reshape(nb, 1, D_MODEL),
            k_p.reshape(DEPTH, batch, seq, *hshape), v_p.reshape(DEPTH, batch, seq, *hshape), conv_p,
            k_s.reshape(DEPTH, nb, 1, *hshape), v_s.reshape(DEPTH, nb, 1, *hshape), conv_s)
```

```python
import functools
import math

import jax
import jax.numpy as jnp
from jax import lax
from jax.experimental import pallas as pl
from jax.experimental.pallas import tpu as pltpu

D_MODEL = 1024
N_HEADS = 8
HEAD_DIM = 64
HEAD_W = 2 * HEAD_DIM
CONV_K = 31
CONV_STATE = CONV_K - 1
PLE_DIM = 256
PAGE_SIZE = 128
ROPE_THETA = 10000.0
EPS = 1e-6
DEPTH = 4

COL_Q, COL_K, COL_V, COL_GATT, COL_UA, COL_GCONV, COL_GA, COL_GB = 0, 1, 2, 3, 4, 6, 7, 8

Q_SCALE = HEAD_DIM ** -0.5 * math.log2(math.e)
ATTN_TILE = 512
ROW_TILE = 256
DECODE_PAGES_PER_STEP = 8

VMEM_LIMIT_BYTES = 56 * 1024 * 1024
CONV_HALO = 32
CONV_ROW_CHUNK = 32

f32 = jnp.float32
bf16 = jnp.bfloat16


def _params(*sem):
    return pltpu.CompilerParams(dimension_semantics=sem, vmem_limit_bytes=VMEM_LIMIT_BYTES)


def _dot(a, b):
    return jnp.dot(a, b, preferred_element_type=f32)


def _dot_nt(a, b):
    return lax.dot_general(a, b, (((1,), (1,)), ((), ())), preferred_element_type=f32)


def _sigmoid(z):
    return 1.0 / (1.0 + jnp.exp(-z))


def _silu(z):
    return z * _sigmoid(z)


def _rms(xf, w):
    return xf * lax.rsqrt(jnp.mean(xf * xf, axis=-1, keepdims=True) + EPS) * w


def _row_spec(tm, width=D_MODEL):
    return pl.BlockSpec((tm, width), lambda i: (i, 0))


def _const_spec(shape, idx=(0, 0)):
    return pl.BlockSpec(shape, lambda i: idx)


def _norm_kernel(x_ref, w_ref, h_ref):
    h_ref[...] = _rms(x_ref[...], w_ref[...]).astype(bf16)


def _prenorm(x, w, tm):
    n = x.shape[0]
    return pl.pallas_call(
        _norm_kernel,
        grid=(n // tm,),
        in_specs=[_row_spec(tm), _const_spec((1, D_MODEL))],
        out_specs=_row_spec(tm),
        out_shape=jax.ShapeDtypeStruct((n, D_MODEL), bf16),
        compiler_params=_params("parallel"),
        name="prenorm",
    )(x, w)


def _rope(z, cos_ref, sa_ref, sb_ref):
    wide = lambda t_ref: jnp.concatenate([t_ref[...]] * N_HEADS, axis=1)
    return (z * wide(cos_ref)
            + pltpu.roll(z, D_MODEL - HEAD_DIM // 2, 1) * wide(sa_ref)
            + pltpu.roll(z, HEAD_DIM // 2, 1) * wide(sb_ref))


def _q_kernel(h_ref, w_ref, cos_ref, sa_ref, sb_ref, q1_ref, q2_ref, *, transposed):
    r = _rope(_dot(h_ref[...], w_ref[...]), cos_ref, sa_ref, sb_ref) * Q_SCALE
    if transposed:
        r = r.T
    idx = lax.broadcasted_iota(jnp.int32, r.shape, 0 if transposed else 1)
    first = (idx & HEAD_DIM) == 0
    q1_ref[...] = jnp.where(first, r, 0.0).astype(bf16)
    q2_ref[...] = jnp.where(first, 0.0, r).astype(bf16)


def _store_heads(o_ref, z):
    for hd in range(N_HEADS):
        o_ref[:, hd, :] = z[:, hd * HEAD_W:(hd + 1) * HEAD_W]


def _k_kernel(h_ref, w_ref, cos_ref, sa_ref, sb_ref, k_ref, kb_ref):
    r = _rope(_dot(h_ref[...], w_ref[...]), cos_ref, sa_ref, sb_ref)
    _store_heads(k_ref, r)
    kb_ref[...] = r.astype(bf16)


def _v_kernel(h_ref, w_ref, v_ref, vt_ref):
    z = _dot(h_ref[...], w_ref[...])
    _store_heads(v_ref, z)
    vt_ref[...] = z.T.astype(bf16)


def _v_rows_kernel(h_ref, w_ref, v_ref):
    _store_heads(v_ref, _dot(h_ref[...], w_ref[...]))


def _silu_kernel(h_ref, w_ref, o_ref):
    o_ref[...] = _silu(_dot(h_ref[...], w_ref[...])).astype(bf16)


def _glu_kernel(h_ref, w_ref, u_ref):
    z = _dot(h_ref[...], w_ref[...])
    u_ref[...] = z[:, :D_MODEL] * _sigmoid(z[:, D_MODEL:])


def _project(kernel_fn, name, h, w_in_b, col, tm, outs, tables=None, wcols=D_MODEL):
    n = h.shape[0]
    in_specs = [_row_spec(tm), _const_spec((D_MODEL, wcols), (0, col * D_MODEL // wcols))]
    args = [h, w_in_b]
    if tables is not None:
        tb = tables[0].shape[0] // tm
        in_specs += [pl.BlockSpec((tm, HEAD_W), lambda i: (i % tb, 0))] * 3
        args += list(tables)
    spec = dict(rows=_row_spec(tm),
                heads=pl.BlockSpec((tm, N_HEADS, HEAD_W), lambda i: (i, 0, 0)),
                cols=pl.BlockSpec((None, D_MODEL, tm), lambda i: (i, 0, 0)))
    shape = dict(rows=(n, D_MODEL), heads=(n, N_HEADS, HEAD_W), cols=(n // tm, D_MODEL, tm))
    return pl.pallas_call(
        kernel_fn,
        grid=(n // tm,),
        in_specs=in_specs,
        out_specs=[spec[lay] for lay, _ in outs],
        out_shape=[jax.ShapeDtypeStruct(shape[lay], dt) for lay, dt in outs],
        compiler_params=_params("parallel"),
        name=name,
    )(*args)


def _lambda(lq1_ref, lk1_ref, lq2_ref, lk2_ref, lam_init):
    a = jnp.sum(lq1_ref[...] * lk1_ref[...], axis=-1, keepdims=True)
    b = jnp.sum(lq2_ref[...] * lk2_ref[...], axis=-1, keepdims=True)
    return jnp.exp(a) - jnp.exp(b) + lam_init


def _attn_kernel(q1t_ref, q2t_ref, k_ref, vt_ref, g_ref, sw_ref, lq1_ref, lk1_ref, lq2_ref, lk2_ref,
                 o_ref, m_ref, l_ref, acc_ref, sa_ref, sb_ref, *, lam_init, tile):
    qi = pl.program_id(2)
    m_ref[...] = jnp.full(m_ref.shape, -jnp.inf, f32)
    l_ref[...] = jnp.zeros(l_ref.shape, f32)
    acc_ref[...] = jnp.zeros(acc_ref.shape, f32)

    def scores(j, s_ref):
        kc = k_ref[pl.ds(pl.multiple_of(j * tile, tile), tile), :]
        s_ref[0] = _dot(kc, q1t_ref[...])
        s_ref[1] = _dot(kc, q2t_ref[...])

    def update(j, s_ref, masked):
        vc = vt_ref[j]
        for c in range(2):
            s = s_ref[c]
            if masked:
                key = lax.broadcasted_iota(jnp.int32, s.shape, 0)
                qry = lax.broadcasted_iota(jnp.int32, s.shape, 1)
                s = jnp.where(key <= qry, s, -jnp.inf)
            m_prev = m_ref[c]
            m_new = jnp.maximum(m_prev, jnp.max(s, axis=0, keepdims=True))
            alpha = jnp.exp2(m_prev - m_new)
            p = jnp.exp2(s - m_new)
            l_ref[c] = alpha * l_ref[c] + jnp.sum(p, axis=0, keepdims=True)
            acc_ref[c] = alpha * acc_ref[c] + _dot(vc, p.astype(bf16))
            m_ref[c] = m_new

    def pair(t, carry):
        j = 2 * t
        scores(j + 1, sb_ref)
        update(j, sa_ref, False)
        scores(j + 2, sa_ref)
        update(j + 1, sb_ref, False)
        return carry

    scores(0, sa_ref)
    lax.fori_loop(0, lax.shift_right_logical(qi, 1), pair, 0)

    @pl.when((qi & 1) == 0)
    def _():
        update(qi, sa_ref, True)

    @pl.when((qi & 1) == 1)
    def _():
        scores(qi, sb_ref)
        update(qi - 1, sa_ref, False)
        update(qi, sb_ref, True)

    lam = _lambda(lq1_ref, lk1_ref, lq2_ref, lk2_ref, lam_init)
    ot = acc_ref[0] * (1.0 / l_ref[0]) - lam * (acc_ref[1] * (1.0 / l_ref[1]))
    ot = ot * lax.rsqrt(jnp.mean(ot * ot, axis=0, keepdims=True) + EPS)
    o = ot.T * sw_ref[...] * (1.0 - lam_init)
    o_ref[...] = (o * g_ref[...].astype(f32)).astype(bf16)


def _decode_body(step, q_ref, kn_ref, vn_ref, g_ref, sw_ref, lq1_ref, lk1_ref, lq2_ref, lk2_ref,
                 *refs, lam_init, n_steps, group):
    kp_refs, vp_refs = refs[:group], refs[group:2 * group]
    o_ref, m_ref, l_ref, acc_ref = refs[2 * group:]
    rows = PAGE_SIZE * N_HEADS

    @pl.when(step == 0)
    def _():
        m_ref[...] = jnp.full(m_ref.shape, -jnp.inf, f32)
        l_ref[...] = jnp.zeros(l_ref.shape, f32)
        acc_ref[...] = jnp.zeros(acc_ref.shape, f32)

    q = q_ref[...].astype(bf16)
    s = jnp.concatenate([_dot_nt(q, kp[...].reshape(rows, HEAD_W).astype(bf16)) for kp in kp_refs],
                        axis=1)
    r_head = lax.broadcasted_iota(jnp.int32, s.shape, 0) % N_HEADS
    c_head = lax.broadcasted_iota(jnp.int32, s.shape, 1) % N_HEADS
    s = jnp.where(r_head == c_head, s, -jnp.inf)
    m_prev = m_ref[...]
    m_new = jnp.maximum(m_prev, jnp.max(s, axis=-1, keepdims=True))
    alpha = jnp.exp2(m_prev - m_new)
    p = jnp.exp2(s - m_new)
    l_ref[...] = alpha * l_ref[...] + jnp.sum(p, axis=-1, keepdims=True)
    pb = p.astype(bf16)
    pv = _dot(pb[:, :rows], vp_refs[0][...].reshape(rows, HEAD_W).astype(bf16))
    for gi in range(1, group):
        pv = pv + _dot(pb[:, gi * rows:(gi + 1) * rows], vp_refs[gi][...].reshape(rows, HEAD_W).astype(bf16))
    acc_ref[...] = alpha * acc_ref[...] + pv
    m_ref[...] = m_new

    @pl.when(step == n_steps - 1)
    def _():
        kn = kn_ref[...]
        vn = vn_ref[...]
        s_n = jnp.sum(q_ref[...] * jnp.concatenate([kn, kn], axis=0), axis=-1, keepdims=True)
        m_prev = m_ref[...]
        m_new = jnp.maximum(m_prev, s_n)
        alpha = jnp.exp2(m_prev - m_new)
        p_n = jnp.exp2(s_n - m_new)
        l = alpha * l_ref[...] + p_n
        acc = (alpha * acc_ref[...] + p_n * jnp.concatenate([vn, vn], axis=0)) / l
        lam = _lambda(lq1_ref, lk1_ref, lq2_ref, lk2_ref, lam_init)
        o = acc[:N_HEADS] - lam * acc[N_HEADS:]
        o = _rms(o, sw_ref[...]) * (1.0 - lam_init)
        o_ref[...] = o * g_ref[...]


N_PROMPT_IN = 10
N_PROMPT_SCRATCH = 5


def _fused_attn_kernel(pt_ref, *refs, lam_init, tile, nq, n_steps, group):
    n_dec_in = 4 + 2 * group
    p_in = refs[:N_PROMPT_IN]
    d_in = refs[N_PROMPT_IN:N_PROMPT_IN + n_dec_in]
    o_ref, od_ref = refs[N_PROMPT_IN + n_dec_in:N_PROMPT_IN + n_dec_in + 2]
    scratch = refs[N_PROMPT_IN + n_dec_in + 2:]
    _attn_kernel(*p_in, o_ref, *scratch[:N_PROMPT_SCRATCH], lam_init=lam_init, tile=tile)
    lin = (pl.program_id(0) * N_HEADS + pl.program_id(1)) * nq + pl.program_id(2)
    _decode_body(lin % n_steps, *d_in[:4], *p_in[5:], *d_in[4:], od_ref, *scratch[N_PROMPT_SCRATCH:],
                 lam_init=lam_init, n_steps=n_steps, group=group)


def _fused_attention(page_table, q1t, q2t, kb, vt, g, sw, lams, dq, dkn, dvn, dg, cache_k, cache_v,
                     layer, lam_init, batch, seq, tile):
    nq = seq // tile
    nb, n_pages = page_table.shape
    group = DECODE_PAGES_PER_STEP
    n_steps = n_pages // group
    assert batch * N_HEADS * nq == nb * n_steps

    def lin(b, h, qi):
        return (b * N_HEADS + h) * nq + qi

    qspec = pl.BlockSpec((None, HEAD_W, tile), lambda b, h, qi, pt: (b * nq + qi, h, 0))
    kspec = pl.BlockSpec((seq, HEAD_W), lambda b, h, qi, pt: (b, h))
    vspec = pl.BlockSpec((nq, HEAD_W, tile), lambda b, h, qi, pt: (b, h, 0))
    ospec = pl.BlockSpec((tile, HEAD_W), lambda b, h, qi, pt: (b * nq + qi, h))
    small = lambda w: pl.BlockSpec((1, w), lambda b, h, qi, pt: (0, 0))
    per_row = lambda r: pl.BlockSpec((None, r, HEAD_W), lambda b, h, qi, pt: (lin(b, h, qi) // n_steps, 0, 0))

    def page(gi):
        return pl.BlockSpec(
            (None, None, PAGE_SIZE, N_HEADS, HEAD_W),
            lambda b, h, qi, pt: (layer, pt[lin(b, h, qi) // n_steps, (lin(b, h, qi) % n_steps) * group + gi],
                                  0, 0, 0))

    pages = [page(gi) for gi in range(group)]
    grid_spec = pltpu.PrefetchScalarGridSpec(
        num_scalar_prefetch=1,
        grid=(batch, N_HEADS, nq),
        in_specs=[qspec, qspec, kspec, vspec, ospec, small(HEAD_W)] + [small(HEAD_DIM)] * 4
                 + [per_row(2 * N_HEADS), per_row(N_HEADS), per_row(N_HEADS), per_row(N_HEADS)] + pages + pages,
        out_specs=[ospec, per_row(N_HEADS)],
        scratch_shapes=[pltpu.VMEM((2, 1, tile), f32), pltpu.VMEM((2, 1, tile), f32),
                        pltpu.VMEM((2, HEAD_W, tile), f32),
                        pltpu.VMEM((2, tile, tile), f32), pltpu.VMEM((2, tile, tile), f32),
                        pltpu.VMEM((2 * N_HEADS, 1), f32), pltpu.VMEM((2 * N_HEADS, 1), f32),
                        pltpu.VMEM((2 * N_HEADS, HEAD_W), f32)],
    )
    return pl.pallas_call(
        functools.partial(_fused_attn_kernel, lam_init=lam_init, tile=tile, nq=nq, n_steps=n_steps, group=group),
        grid_spec=grid_spec,
        out_shape=[jax.ShapeDtypeStruct((batch * seq, D_MODEL), bf16),
                   jax.ShapeDtypeStruct((nb, N_HEADS, HEAD_W), f32)],
        compiler_params=_params("arbitrary", "arbitrary", "arbitrary"),
        name="fused_attention",
    )(page_table, q1t, q2t, kb, vt, g, sw, *lams, dq, dkn, dvn, dg,
      *([cache_k] * group), *([cache_v] * group))


def _conv_tail(y, h_ref, wg_ref, g_ref, b_ref, o_ref):
    mu = jnp.mean(y, axis=-1, keepdims=True)
    yc = y - mu
    var = jnp.mean(yc * yc, axis=-1, keepdims=True)
    c = _silu(yc * lax.rsqrt(var + EPS) * g_ref[...] + b_ref[...])
    o_ref[...] = (c * _silu(_dot(h_ref[...], wg_ref[...]))).astype(bf16)


def _conv_kernel(u_ref, halo_ref, h_ref, wg_ref, cw_ref, cb_ref, g_ref, b_ref, o_ref,
                 full_ref, shift_ref, y_ref, *, tm, tiles_per_seq):
    i = pl.program_id(0)
    full_ref[0:CONV_HALO, :] = jnp.where(i % tiles_per_seq == 0, 0.0, halo_ref[...])
    full_ref[CONV_HALO:, :] = u_ref[...]
    span = tm + CONV_HALO - 8
    for s in range(1, 8):
        shift_ref[s - 1] = full_ref[s:s + span, :]

    def body(rb, carry):
        r0 = pl.multiple_of(rb * CONV_ROW_CHUNK, CONV_ROW_CHUNK)
        acc = jnp.broadcast_to(cb_ref[...], (CONV_ROW_CHUNK, D_MODEL))
        for k in range(CONV_K):
            off = CONV_HALO - CONV_STATE + k
            s, a = off % 8, off - off % 8
            if s == 0:
                tap = full_ref[pl.ds(r0 + a, CONV_ROW_CHUNK), :]
            else:
                tap = shift_ref[s - 1, pl.ds(r0 + a, CONV_ROW_CHUNK), :]
            acc = acc + cw_ref[k:k + 1, :] * tap
        y_ref[pl.ds(r0, CONV_ROW_CHUNK), :] = acc
        return carry

    lax.fori_loop(0, tm // CONV_ROW_CHUNK, body, 0)
    _conv_tail(y_ref[...], h_ref, wg_ref, g_ref, b_ref, o_ref)


def _prompt_conv(u, h, w_in_b, cw, cb, g, b, seq, tm):
    n = u.shape[0]
    ratio = tm // CONV_HALO
    halo = pl.BlockSpec((CONV_HALO, D_MODEL), lambda i: (jnp.maximum(i * ratio - 1, 0), 0))
    return pl.pallas_call(
        functools.partial(_conv_kernel, tm=tm, tiles_per_seq=seq // tm),
        grid=(n // tm,),
        in_specs=[_row_spec(tm), halo, _row_spec(tm), _const_spec((D_MODEL, D_MODEL), (0, COL_GCONV)),
                  _const_spec((CONV_K, D_MODEL))] + [_const_spec((1, D_MODEL))] * 3,
        out_specs=_row_spec(tm),
        out_shape=jax.ShapeDtypeStruct((n, D_MODEL), bf16),
        scratch_shapes=[pltpu.VMEM((tm + CONV_HALO, D_MODEL), f32),
                        pltpu.VMEM((7, tm + CONV_HALO - 8, D_MODEL), f32),
                        pltpu.VMEM((tm, D_MODEL), f32)],
        compiler_params=_params("parallel"),
        name="prompt_conv",
    )(u, u, h, w_in_b, cw, cb, g, b)


def _sample_conv_kernel(st_ref, u_ref, h_ref, wg_ref, cw_ref, cb_ref, g_ref, b_ref, o_ref):
    w = cw_ref[...]
    y = jnp.sum(st_ref[...] * w[None, :CONV_STATE, :], axis=1)
    y = y + u_ref[...] * w[CONV_STATE:, :] + cb_ref[...]
    _conv_tail(y, h_ref, wg_ref, g_ref, b_ref, o_ref)


def _sample_conv(state, u, h, w_in_b, cw, cb, g, b):
    nb = u.shape[0]
    return pl.pallas_call(
        _sample_conv_kernel,
        grid=(1,),
        in_specs=[pl.BlockSpec((nb, CONV_STATE, D_MODEL), lambda i: (0, 0, 0)), _row_spec(nb), _row_spec(nb),
                  _const_spec((D_MODEL, D_MODEL), (0, COL_GCONV)), _const_spec((CONV_K, D_MODEL))]
                 + [_const_spec((1, D_MODEL))] * 3,
        out_specs=_row_spec(nb),
        out_shape=jax.ShapeDtypeStruct((nb, D_MODEL), bf16),
        compiler_params=_params("arbitrary"),
        name="sample_conv",
    )(state, u, h, w_in_b, cw, cb, g, b)


def _merge_kernel(x_ref, h_ref, oa_ref, oc_ref, p_ref, wga_ref, wgb_ref, wap_ref, wcp_ref, wout_ref,
                  wple_ref, wpg_ref, npost_ref, npre_ref, xo_ref, ho_ref):
    h = h_ref[...]
    o_att = _dot(oa_ref[...], wap_ref[...])
    o_conv = _dot(oc_ref[...], wcp_ref[...])
    m = _sigmoid(_dot(h, wga_ref[...])) * o_att + _sigmoid(_dot(h, wgb_ref[...])) * o_conv
    x1 = x_ref[...] + _rms(_dot(m.astype(bf16), wout_ref[...]), npost_ref[...])
    ple = _dot(p_ref[...].astype(bf16), wple_ref[...])
    x2 = x1 + ple * _sigmoid(_dot(x1.astype(bf16), wpg_ref[...]))
    xo_ref[...] = x2
    ho_ref[...] = _rms(x2, npre_ref[...]).astype(bf16)


def _merge(x, h, oa, oc, p, w_in_b, wap, wcp, wout, wple, wpg, npost, npre_next, tm):
    n = x.shape[0]
    sq = _const_spec((D_MODEL, D_MODEL))
    return pl.pallas_call(
        _merge_kernel,
        grid=(n // tm,),
        in_specs=[_row_spec(tm)] * 4 + [_row_spec(tm, PLE_DIM),
                  _const_spec((D_MODEL, D_MODEL), (0, COL_GA)), _const_spec((D_MODEL, D_MODEL), (0, COL_GB)),
                  sq, sq, sq, _const_spec((PLE_DIM, D_MODEL)), sq,
                  _const_spec((1, D_MODEL)), _const_spec((1, D_MODEL))],
        out_specs=[_row_spec(tm), _row_spec(tm)],
        out_shape=[jax.ShapeDtypeStruct((n, D_MODEL), f32), jax.ShapeDtypeStruct((n, D_MODEL), bf16)],
        compiler_params=_params("parallel"),
        name="merge",
    )(x, h, oa, oc, p, w_in_b, w_in_b, wap, wcp, wout, wple, wpg, npost, npre_next)


def _rope_tables(pos, rows):
    inv = 1.0 / (ROPE_THETA ** (jnp.arange(0, HEAD_DIM, 2, dtype=f32) / HEAD_DIM))
    ang = pos.astype(f32)[:, None] * inv[None, :]
    ang = jnp.concatenate([ang, ang], axis=-1)
    reps = HEAD_W // HEAD_DIM
    cos = jnp.tile(jnp.cos(ang), (1, reps))
    sin = jnp.tile(jnp.sin(ang), (1, reps))
    first = (jnp.arange(HEAD_W) % HEAD_DIM) < HEAD_DIM // 2
    tables = (cos, jnp.where(first, -sin, 0.0), jnp.where(first, 0.0, sin))
    return tuple(jnp.broadcast_to(t, (rows, HEAD_W)) for t in tables)


def _layer_front(h, w_in_b, tables, tm, prompt):
    q_lay = "cols" if prompt else "rows"
    q1, q2 = _project(functools.partial(_q_kernel, transposed=prompt), "proj_q", h, w_in_b, COL_Q, tm,
                      ((q_lay, bf16), (q_lay, bf16)), tables)
    k, kb = _project(_k_kernel, "proj_k", h, w_in_b, COL_K, tm, (("heads", f32), ("rows", bf16)), tables)
    if prompt:
        v, vt = _project(_v_kernel, "proj_v", h, w_in_b, COL_V, tm, (("heads", f32), ("cols", bf16)))
    else:
        (v,), vt = _project(_v_rows_kernel, "proj_v", h, w_in_b, COL_V, tm, (("heads", f32),)), None
    (g_att,) = _project(_silu_kernel, "proj_gatt", h, w_in_b, COL_GATT, tm, (("rows", bf16),))
    (u,) = _project(_glu_kernel, "proj_glu", h, w_in_b, COL_UA, tm, (("rows", f32),), wcols=2 * D_MODEL)
    return dict(q1=q1, q2=q2, k=k, kb=kb, v=v, vt=vt, g_att=g_att, u=u)


def _layer_back(l, x, h, oa, u, p_l, conv_fn, wts, tm):
    w_in_b = wts["w_in"][l]
    oc = conv_fn(l, u, h, w_in_b, wts["conv_w"][l], wts["conv_b"][l][None],
                 wts["conv_ln_g"][l][None], wts["conv_ln_b"][l][None])
    return _merge(x, h, oa, oc, p_l, w_in_b, wts["w_attn_proj"][l], wts["w_conv_proj"][l],
                  wts["w_out"][l], wts["w_ple"][l], wts["w_ple_gate"][l],
                  wts["norm_post"][l][None], wts["norm_pre"][(l + 1) % DEPTH][None], tm)


def kernel(x_prompt, x_sample, cache_k, cache_v, state_conv, page_table, p_prompt, p_sample,
           norm_pre, norm_post, w_in, lambda_q1, lambda_k1, lambda_q2, lambda_k2, subln_w,
           w_attn_proj, conv_w, conv_b, conv_ln_g, conv_ln_b, w_conv_proj, w_out, w_ple, w_ple_gate):
    batch, seq, _ = x_prompt.shape
    nb, dec_seq, _ = x_sample.shape
    assert dec_seq == 1
    n_pages = page_table.shape[1]
    past = n_pages * PAGE_SIZE
    wts = dict(norm_pre=norm_pre, norm_post=norm_post, w_in=w_in.astype(bf16),
               lambda_q1=lambda_q1, lambda_k1=lambda_k1, lambda_q2=lambda_q2, lambda_k2=lambda_k2,
               subln_w=subln_w, w_attn_proj=w_attn_proj.astype(bf16), conv_w=conv_w, conv_b=conv_b,
               conv_ln_g=conv_ln_g, conv_ln_b=conv_ln_b, w_conv_proj=w_conv_proj.astype(bf16),
               w_out=w_out.astype(bf16), w_ple=w_ple.astype(bf16), w_ple_gate=w_ple_gate.astype(bf16))

    assert seq % ATTN_TILE == 0 and ATTN_TILE % ROW_TILE == 0 and n_pages % DECODE_PAGES_PER_STEP == 0

    def conv_prompt_fn(l, u, h, w_in_b, cw, cb, g, b):
        return _prompt_conv(u, h, w_in_b, cw, cb, g, b, seq, ROW_TILE)

    def conv_sample_fn(l, u, h, w_in_b, cw, cb, g, b):
        return _sample_conv(state_conv[l], u, h, w_in_b, cw, cb, g, b)

    tables_p = _rope_tables(jnp.arange(seq, dtype=jnp.int32), seq)
    tables_s = _rope_tables(past + jnp.arange(1, dtype=jnp.int32), nb)
    pp = p_prompt.reshape(DEPTH, batch * seq, PLE_DIM)
    ps = p_sample.reshape(DEPTH, nb, PLE_DIM)
    x_p = x_prompt.reshape(batch * seq, D_MODEL)
    x_s = x_sample.reshape(nb, D_MODEL)
    h_p = _prenorm(x_p, norm_pre[0][None], ROW_TILE)
    h_s = _prenorm(x_s, norm_pre[0][None], nb)
    per_head = lambda a: a.astype(f32).reshape(nb, N_HEADS, HEAD_W)
    outs = dict(k_p=[], v_p=[], u_p=[], k_s=[], v_s=[], u_s=[])
    for l in range(DEPTH):
        lam_init = 0.8 - 0.6 * math.exp(-0.3 * l)
        f_p = _layer_front(h_p, wts["w_in"][l], tables_p, ATTN_TILE, True)
        f_s = _layer_front(h_s, wts["w_in"][l], tables_s, nb, False)
        lams = tuple(wts[n][l][None] for n in ("lambda_q1", "lambda_k1", "lambda_q2", "lambda_k2"))
        dq = jnp.concatenate([per_head(f_s["q1"]), per_head(f_s["q2"])], axis=1)
        oa_p, oa_s = _fused_attention(page_table, f_p["q1"], f_p["q2"], f_p["kb"], f_p["vt"], f_p["g_att"],
                                      subln_w[l][None], lams, dq, f_s["k"], f_s["v"], per_head(f_s["g_att"]),
                                      cache_k, cache_v, l, lam_init, batch, seq, ATTN_TILE)
        x_p, h_p = _layer_back(l, x_p, h_p, oa_p, f_p["u"], pp[l], conv_prompt_fn, wts, ROW_TILE)
        x_s, h_s = _layer_back(l, x_s, h_s, oa_s.reshape(nb, D_MODEL).astype(bf16), f_s["u"], ps[l],
                               conv_sample_fn, wts, nb)
        for name, f in (("p", f_p), ("s", f_s)):
            outs["k_" + name].append(f["k"])
            outs["v_" + name].append(f["v"])
            outs["u_" + name].append(f["u"])

    conv_p = jnp.stack([u.reshape(batch, seq, D_MODEL)[:, seq - CONV_STATE:] for u in outs["u_p"]])
    conv_s = jnp.stack([jnp.concatenate([state_conv[l][:, 1:], outs["u_s"][l][:, None]], axis=1)
                        for l in range(DEPTH)])
    hshape = (N_HEADS, HEAD_W)
    stack = lambda name, lead: jnp.stack(outs[name]).reshape(DEPTH, *lead, *hshape)
    return (x_p.reshape(batch, seq, D_MODEL), x_s.reshape(nb, 1, D_MODEL),
            stack("k_p", (batch, seq)), stack("v_p", (batch, seq)), conv_p,
            stack("k_s", (nb, 1)), stack("v_s", (nb, 1)), conv_s)
```
